```python
import math
import jax
import jax.numpy as jnp
from jax import lax
import numpy as np

D_MODEL = 1024
BATCH = 8
SEQ = 4096
DEPTH = 2

CHUNK = 64
MEM_TOKENS = 256
HEAD_DIM = 64
D_MEM = D_MODEL // 4
N_MEM_HEADS = 4
MEM_HEAD_DIM = D_MEM // N_MEM_HEADS
D_TOK = D_MODEL - D_MEM
S5_GROUP = 16
S5_GROUPS = D_TOK // S5_GROUP
S5_STATE = 64
N_FOX_HEADS = D_TOK // HEAD_DIM
Q_BLOCK = 128
D_FF = ((8 * D_MODEL // 3 + 127) // 128) * 128
CONV_W = 3
N_A_LAYERS = DEPTH // 2
N_B_LAYERS = DEPTH - N_A_LAYERS
EPS = 1e-6

kernel_name = "hybrid_s5_fox_yoco_encoder"


def rmsnorm(x, g):
    xf = x.astype(jnp.float32)
    y = xf * lax.rsqrt(jnp.mean(xf * xf, axis=-1, keepdims=True) + EPS)
    return (y * g.astype(jnp.float32)).astype(x.dtype)


def s5_mixer(u, a_re, a_im, log_dt, b_re, b_im, c_re, c_im, d_skip):
    f32 = jnp.float32
    bsz, seq, _ = u.shape
    n_chunks = seq // CHUNK
    uf = u.astype(f32)
    u_c = uf.reshape(bsz, n_chunks, CHUNK, S5_GROUPS, S5_GROUP).transpose(1, 2, 0, 3, 4)
    dt = jnp.exp(log_dt.astype(f32))[:, None]
    lam_re = jnp.minimum(a_re.astype(f32), -1e-4)
    lam_im = a_im.astype(f32)
    mag = jnp.exp(lam_re * dt)
    ph = lam_im * dt
    ab_re, ab_im = mag * jnp.cos(ph), mag * jnp.sin(ph)
    den = lam_re * lam_re + lam_im * lam_im
    z_re = ((ab_re - 1.0) * lam_re + ab_im * lam_im) / den
    z_im = (ab_im * lam_re - (ab_re - 1.0) * lam_im) / den
    br, bi = b_re.astype(f32), b_im.astype(f32)
    bb_re = z_re[..., None] * br - z_im[..., None] * bi
    bb_im = z_re[..., None] * bi + z_im[..., None] * br
    steps = jnp.arange(1, CHUNK + 1, dtype=f32)[:, None, None]
    pw_mag = jnp.exp(lam_re * dt * steps)
    pw_ph = lam_im * dt * steps
    pw_re, pw_im = pw_mag * jnp.cos(pw_ph), pw_mag * jnp.sin(pw_ph)
    pw_re, pw_im = pw_re[:, None], pw_im[:, None]
    a_el_re = jnp.broadcast_to(ab_re, (CHUNK, bsz, S5_GROUPS, S5_STATE))
    a_el_im = jnp.broadcast_to(ab_im, (CHUNK, bsz, S5_GROUPS, S5_STATE))
    cr, ci = c_re.astype(f32), c_im.astype(f32)

    def combine(e1, e2):
        a1r, a1i, b1r, b1i = e1
        a2r, a2i, b2r, b2i = e2
        return (a2r * a1r - a2i * a1i, a2r * a1i + a2i * a1r,
                a2r * b1r - a2i * b1i + b2r, a2r * b1i + a2i * b1r + b2i)

    def chunk_step(carry, uc):
        hp_re, hp_im = carry
        bu_re = jnp.einsum('lbgi,gpi->lbgp', uc, bb_re)
        bu_im = jnp.einsum('lbgi,gpi->lbgp', uc, bb_im)
        _, _, h_re, h_im = lax.associative_scan(combine, (a_el_re, a_el_im, bu_re, bu_im), axis=0)
        h_re = h_re + pw_re * hp_re[None] - pw_im * hp_im[None]
        h_im = h_im + pw_re * hp_im[None] + pw_im * hp_re[None]
        y = jnp.einsum('lbgp,gip->lbgi', h_re, cr) - jnp.einsum('lbgp,gip->lbgi', h_im, ci)
        return (h_re[-1], h_im[-1]), y

    h0 = jnp.zeros((bsz, S5_GROUPS, S5_STATE), f32)
    _, y = lax.scan(chunk_step, (h0, h0), u_c)
    y = y.transpose(2, 0, 1, 3, 4).reshape(bsz, seq, D_TOK) + d_skip.astype(f32) * uf
    return y.astype(u.dtype)


def memory_attention(q, mem_n, w_mem_kv):
    bsz, seq, _ = q.shape
    kv = mem_n @ w_mem_kv
    k, v = jnp.split(kv, 2, axis=-1)
    k = k.reshape(bsz, MEM_TOKENS, N_MEM_HEADS, MEM_HEAD_DIM)
    v = v.reshape(bsz, MEM_TOKENS, N_MEM_HEADS, MEM_HEAD_DIM)
    qh = q.reshape(bsz, seq, N_MEM_HEADS, MEM_HEAD_DIM)
    s = jnp.einsum('bshd,bmhd->bhsm', qh, k).astype(jnp.float32) * (MEM_HEAD_DIM ** -0.5)
    p = jax.nn.softmax(s, axis=-1)
    o = jnp.einsum('bhsm,bmhd->bshd', p.astype(v.dtype), v)
    return o.reshape(bsz, seq, D_MEM)


def forgetting_attention(q, k, v, fcum):
    bsz, seq, _, _ = q.shape
    scale = HEAD_DIM ** -0.5
    outs = []
    for i in range(seq // Q_BLOCK):
        q0, q1 = i * Q_BLOCK, (i + 1) * Q_BLOCK
        qs = q[:, q0:q1]
        s = jnp.einsum('bqhd,bkhd->bhqk', qs, k[:, :q1]).astype(jnp.float32) * scale
        s = s + fcum[:, :, q0:q1, None] - fcum[:, :, None, :q1]
        qpos = q0 + jnp.arange(Q_BLOCK)
        kpos = jnp.arange(q1)
        s = jnp.where(qpos[:, None] >= kpos[None, :], s, -jnp.inf)
        p = jax.nn.softmax(s, axis=-1)
        outs.append(jnp.einsum('bhqk,bkhd->bqhd', p.astype(v.dtype), v[:, :q1]))
    return jnp.concatenate(outs, axis=1).reshape(bsz, seq, D_TOK)


def conv_ffn(x, w_up, conv_w, conv_b, w_down):
    seq = x.shape[1]
    up = x @ w_up
    a, g = jnp.split(up, 2, axis=-1)
    gp = jnp.pad(g, ((0, 0), (CONV_W - 1, 0), (0, 0)))
    gc = gp[:, 0:seq] * conv_w[0] + gp[:, 1:seq + 1] * conv_w[1] + gp[:, 2:seq + 2] * conv_w[2] + conv_b
    return (jax.nn.silu(gc) * a) @ w_down


def setup_inputs(seed: int = 0) -> dict:
    key = jax.random.key(seed)
    ks = jax.random.split(key, 32)
    f32 = jnp.float32
    nrm = lambda k, shape, s: jax.random.normal(k, shape, f32) * s
    n_idx = jnp.arange(S5_STATE, dtype=f32)
    return {
        "x": nrm(ks[0], (BATCH, SEQ, D_MODEL), 1.0),
        "mem": nrm(ks[1], (BATCH, MEM_TOKENS, D_MODEL), 1.0),
        "g_mix": 1.0 + nrm(ks[2], (DEPTH, D_MODEL), 0.02),
        "w_in": nrm(ks[3], (DEPTH, D_MODEL, D_MODEL), D_MODEL ** -0.5),
        "w_out": nrm(ks[4], (DEPTH, D_MODEL, D_MODEL), D_MODEL ** -0.5),
        "g_mem": 1.0 + nrm(ks[5], (D_MODEL,), 0.02),
        "w_mem_kv": nrm(ks[6], (DEPTH, D_MODEL, 2 * D_MEM), D_MODEL ** -0.5),
        "s5_a_re": -0.5 + nrm(ks[7], (N_A_LAYERS, S5_GROUPS, S5_STATE), 0.01),
        "s5_a_im": math.pi * n_idx + nrm(ks[8], (N_A_LAYERS, S5_GROUPS, S5_STATE), 0.01),
        "s5_log_dt": jax.random.uniform(ks[9], (N_A_LAYERS, S5_GROUPS), f32, math.log(1e-3), math.log(1e-1)),
        "s5_b_re": nrm(ks[10], (N_A_LAYERS, S5_GROUPS, S5_STATE, S5_GROUP), (2 * S5_GROUP) ** -0.5),
        "s5_b_im": nrm(ks[11], (N_A_LAYERS, S5_GROUPS, S5_STATE, S5_GROUP), (2 * S5_GROUP) ** -0.5),
        "s5_c_re": nrm(ks[12], (N_A_LAYERS, S5_GROUPS, S5_GROUP, S5_STATE), S5_STATE ** -0.5),
        "s5_c_im": nrm(ks[13], (N_A_LAYERS, S5_GROUPS, S5_GROUP, S5_STATE), S5_STATE ** -0.5),
        "s5_d": nrm(ks[14], (N_A_LAYERS, D_TOK), 1.0),
        "w_glu": nrm(ks[15], (N_A_LAYERS, D_TOK, D_TOK), D_TOK ** -0.5),
        "g_kv": 1.0 + nrm(ks[16], (D_MODEL,), 0.02),
        "w_kv": nrm(ks[17], (D_MODEL, 2 * D_TOK), D_MODEL ** -0.5),
        "w_fgate": nrm(ks[18], (D_MODEL, N_FOX_HEADS), 0.5 * D_MODEL ** -0.5),
        "b_fgate": jax.random.uniform(ks[19], (N_FOX_HEADS,), f32, 1.0, 6.0),
        "g_ffn": 1.0 + nrm(ks[20], (DEPTH, D_MODEL), 0.02),
        "w_ffn_up": nrm(ks[21], (DEPTH, D_MODEL, 2 * D_FF), D_MODEL ** -0.5),
        "conv_w": nrm(ks[22], (DEPTH, CONV_W, D_FF), CONV_W ** -0.5),
        "conv_b": nrm(ks[23], (DEPTH, D_FF), 0.02),
        "w_ffn_down": nrm(ks[24], (DEPTH, D_FF, D_MODEL), D_FF ** -0.5),
        "g_final": 1.0 + nrm(ks[25], (D_MODEL,), 0.02),
    }


def reference(x, mem, g_mix, w_in, w_out, g_mem, w_mem_kv, s5_a_re, s5_a_im, s5_log_dt,
              s5_b_re, s5_b_im, s5_c_re, s5_c_im, s5_d, w_glu, g_kv, w_kv, w_fgate, b_fgate,
              g_ffn, w_ffn_up, conv_w, conv_b, w_ffn_down, g_final):
    bsz, seq, _ = x.shape
    h = x
    mem_n = rmsnorm(mem, g_mem)
    k_sh = v_sh = fcum_sh = None
    for l in range(DEPTH):
        hn = rmsnorm(h, g_mix[l])
        proj = hn @ w_in[l]
        tok_in, q_mem = proj[..., :D_TOK], proj[..., D_TOK:]
        if l < N_A_LAYERS:
            y = s5_mixer(tok_in, s5_a_re[l], s5_a_im[l], s5_log_dt[l], s5_b_re[l], s5_b_im[l],
                         s5_c_re[l], s5_c_im[l], s5_d[l])
            g = jax.nn.gelu(y)
            tok_out = g * jax.nn.sigmoid(g @ w_glu[l])
        else:
            q = tok_in.reshape(bsz, seq, N_FOX_HEADS, HEAD_DIM)
            tok_out = forgetting_attention(q, k_sh, v_sh, fcum_sh)
        mem_out = memory_attention(q_mem, mem_n, w_mem_kv[l])
        h = h + jnp.concatenate([tok_out, mem_out], axis=-1) @ w_out[l]
        h = h + conv_ffn(rmsnorm(h, g_ffn[l]), w_ffn_up[l], conv_w[l], conv_b[l], w_ffn_down[l])
        if l == N_A_LAYERS - 1:
            hs = rmsnorm(h, g_kv)
            kv = hs @ w_kv
            k_sh = kv[..., :D_TOK].reshape(bsz, seq, N_FOX_HEADS, HEAD_DIM)
            v_sh = kv[..., D_TOK:].reshape(bsz, seq, N_FOX_HEADS, HEAD_DIM)
            logf = jax.nn.log_sigmoid((hs @ w_fgate + b_fgate).astype(jnp.float32))
            fcum_sh = jnp.cumsum(logf, axis=1).transpose(0, 2, 1)
    return rmsnorm(h, g_final)
```

```python
import functools
import math

import jax
import jax.numpy as jnp
from jax import lax
from jax.experimental import pallas as pl
from jax.experimental.pallas import tpu as pltpu

F32 = jnp.float32
BF16 = jnp.bfloat16

D_MODEL = 1024
HEAD_DIM = 64
D_MEM = 256
N_MEM_HEADS = 4
MEM_TOKENS = 256
D_TOK = D_MODEL - D_MEM
S5_GROUP = 16
S5_GROUPS = D_TOK // S5_GROUP
S5_STATE = 64
N_FOX_HEADS = D_TOK // HEAD_DIM
D_FF = 2816
EPS = 1e-6

LANES = 128
SUBLANES = 8
MXU_DIM = 256

ROW_TILE = 512
S5_STEPS = MXU_DIM // SUBLANES
S5_SLABS = D_TOK // MXU_DIM
S5_SLAB_STATE = (MXU_DIM // S5_GROUP) * S5_STATE
S5_STATE_ALL = S5_GROUPS * S5_STATE
SCAN_LANES = 512
FF_CHUNK = 256
N_FF_CHUNKS = D_FF // FF_CHUNK
FOX_TQ = 256
FOX_TK = 512
GATE_COLS = 8
KV_TILE = 256
VMEM_LIMIT = 56 * 1024 * 1024


def _rmsnorm(x, g):
    ms = jnp.mean(x * x, axis=-1, keepdims=True)
    return x * lax.rsqrt(ms + EPS) * g


def _const_spec(shape):
    zeros = (0,) * len(shape)
    return pl.BlockSpec(shape, lambda *_: zeros, pipeline_mode=pl.Buffered(1))


def _memkv_kernel(mem_ref, g_ref, w_ref, o_ref):
    mn = _rmsnorm(mem_ref[0], g_ref[...]).astype(BF16)
    for l in range(w_ref.shape[0]):
        o_ref[l, 0] = jnp.dot(mn, w_ref[l], preferred_element_type=F32).astype(BF16)


def _memkv(mem, g_mem, w_mem_kv):
    bsz = mem.shape[0]
    depth = w_mem_kv.shape[0]
    return pl.pallas_call(
        _memkv_kernel,
        grid=(bsz,),
        in_specs=[
            pl.BlockSpec((1, MEM_TOKENS, D_MODEL), lambda b: (b, 0, 0)),
            _const_spec((1, D_MODEL)),
            _const_spec((depth, D_MODEL, 2 * D_MEM)),
        ],
        out_specs=pl.BlockSpec((depth, 1, MEM_TOKENS, 2 * D_MEM), lambda b: (0, b, 0, 0)),
        out_shape=jax.ShapeDtypeStruct((depth, bsz, MEM_TOKENS, 2 * D_MEM), BF16),
        compiler_params=pltpu.CompilerParams(dimension_semantics=("arbitrary",)),
        name="memkv",
    )(mem, g_mem.reshape(1, D_MODEL), w_mem_kv.astype(BF16))


def _rms_proj_kernel(x_ref, g_ref, w_ref, o_ref):
    xn = _rmsnorm(x_ref[...], g_ref[...]).astype(BF16)
    o_ref[...] = jnp.dot(xn, w_ref[...], preferred_element_type=F32).astype(o_ref.dtype)


def _rms_proj(h2d, g, w):
    rows = h2d.shape[0]
    n = w.shape[1]
    return pl.pallas_call(
        _rms_proj_kernel,
        grid=(rows // ROW_TILE,),
        in_specs=[
            pl.BlockSpec((ROW_TILE, D_MODEL), lambda i: (i, 0)),
            _const_spec((1, D_MODEL)),
            _const_spec((D_MODEL, n)),
        ],
        out_specs=pl.BlockSpec((ROW_TILE, n), lambda i: (i, 0)),
        out_shape=jax.ShapeDtypeStruct((rows, n), BF16),
        compiler_params=pltpu.CompilerParams(dimension_semantics=("arbitrary",)),
        name="rms_proj",
    )(h2d, g.reshape(1, D_MODEL), w.astype(BF16))


def _s5_discretise(a_re, a_im, log_dt, b_re, b_im, c_re, c_im):
    dt = jnp.exp(log_dt.astype(F32))[:, None]
    lam_re = jnp.minimum(a_re.astype(F32), -1e-4)
    lam_im = a_im.astype(F32)
    mag = jnp.exp(lam_re * dt)
    ph = lam_im * dt
    ab_re, ab_im = mag * jnp.cos(ph), mag * jnp.sin(ph)
    den = lam_re * lam_re + lam_im * lam_im
    z_re = ((ab_re - 1.0) * lam_re + ab_im * lam_im) / den
    z_im = (ab_im * lam_re - (ab_re - 1.0) * lam_im) / den
    br, bi = b_re.astype(F32), b_im.astype(F32)
    bb_re = z_re[..., None] * br - z_im[..., None] * bi
    bb_im = z_re[..., None] * bi + z_im[..., None] * br
    gps = MXU_DIM // S5_GROUP
    eye = jnp.eye(gps, dtype=F32)

    def pack_b(bb):
        bb = bb.reshape(S5_SLABS, gps, S5_STATE, S5_GROUP)
        return jnp.einsum('sgpi,gh->sgihp', bb, eye).reshape(S5_SLABS, MXU_DIM, S5_SLAB_STATE)

    def pack_c(cc):
        cc = cc.reshape(S5_SLABS, gps, S5_GROUP, S5_STATE)
        return jnp.einsum('sgip,gh->sgphi', cc, eye).reshape(S5_SLABS, S5_SLAB_STATE, MXU_DIM)

    b_bd = jnp.concatenate([pack_b(bb_re), pack_b(bb_im)], axis=2).astype(BF16)
    c_bd = jnp.concatenate([pack_c(c_re.astype(F32)), -pack_c(c_im.astype(F32))], axis=1).astype(BF16)
    return ab_re.reshape(1, S5_STATE_ALL), ab_im.reshape(1, S5_STATE_ALL), b_bd, c_bd


def _s5_glu_kernel(u_ref, perm_ref, permt_ref, bbd_ref, cbd_ref, are_ref, aim_ref, d_ref, wglu_ref,
                   o_ref, hre_ref, him_ref, sre_ref, sim_ref):
    rows = SUBLANES * S5_STEPS

    @pl.when(pl.program_id(0) == 0)
    def _():
        hre_ref[...] = jnp.zeros_like(hre_ref)
        him_ref[...] = jnp.zeros_like(him_ref)

    u = u_ref[...].reshape(rows, D_TOK)
    up = jnp.dot(perm_ref[...], u, preferred_element_type=F32).astype(BF16)

    for s in range(S5_SLABS):
        bu = jnp.dot(up[:, s * MXU_DIM:(s + 1) * MXU_DIM], bbd_ref[s], preferred_element_type=F32)
        sre_ref[:, s * S5_SLAB_STATE:(s + 1) * S5_SLAB_STATE] = bu[:, :S5_SLAB_STATE]
        sim_ref[:, s * S5_SLAB_STATE:(s + 1) * S5_SLAB_STATE] = bu[:, S5_SLAB_STATE:]

    for c in range(S5_STATE_ALL // SCAN_LANES):
        cs = slice(c * SCAN_LANES, (c + 1) * SCAN_LANES)
        ar = jnp.broadcast_to(are_ref[:, cs], (SUBLANES, SCAN_LANES))
        ai = jnp.broadcast_to(aim_ref[:, cs], (SUBLANES, SCAN_LANES))
        hr = hre_ref[:, cs]
        hi = him_ref[:, cs]
        for t in range(S5_STEPS):
            rs = slice(t * SUBLANES, (t + 1) * SUBLANES)
            br = sre_ref[rs, cs]
            bi = sim_ref[rs, cs]
            hr, hi = ar * hr - ai * hi + br, ar * hi + ai * hr + bi
            sre_ref[rs, cs] = hr
            sim_ref[rs, cs] = hi
        hre_ref[:, cs] = hr
        him_ref[:, cs] = hi

    ys = []
    for s in range(S5_SLABS):
        ss = slice(s * S5_SLAB_STATE, (s + 1) * S5_SLAB_STATE)
        hcat = jnp.concatenate([sre_ref[:, ss], sim_ref[:, ss]], axis=1).astype(BF16)
        ys.append(jnp.dot(hcat, cbd_ref[s], preferred_element_type=F32))
    y = jnp.concatenate(ys, axis=1) + d_ref[...] * up.astype(F32)
    g = jax.nn.gelu(y)
    z = jnp.dot(g.astype(BF16), wglu_ref[...], preferred_element_type=F32)
    tok = (g * jax.nn.sigmoid(z)).astype(BF16)
    out = jnp.dot(permt_ref[...], tok, preferred_element_type=F32).astype(BF16)
    o_ref[...] = out.reshape(SUBLANES, S5_STEPS, D_TOK)


def _s5_glu(proj, a_re, a_im, log_dt, b_re, b_im, c_re, c_im, d_skip, w_glu):
    bsz, seq, _ = proj.shape
    assert bsz == SUBLANES, "the scan keeps one batch element per sublane"
    rows = SUBLANES * S5_STEPS
    ab_re, ab_im, b_bd, c_bd = _s5_discretise(a_re, a_im, log_dt, b_re, b_im, c_re, c_im)
    src = (jnp.arange(rows) % SUBLANES) * S5_STEPS + jnp.arange(rows) // SUBLANES
    perm = (src[:, None] == jnp.arange(rows)[None, :]).astype(BF16)
    return pl.pallas_call(
        _s5_glu_kernel,
        grid=(seq // S5_STEPS,),
        in_specs=[
            pl.BlockSpec((bsz, S5_STEPS, D_TOK), lambda i: (0, i, 0)),
            _const_spec((rows, rows)),
            _const_spec((rows, rows)),
            _const_spec((S5_SLABS, MXU_DIM, 2 * S5_SLAB_STATE)),
            _const_spec((S5_SLABS, 2 * S5_SLAB_STATE, MXU_DIM)),
            _const_spec((1, S5_STATE_ALL)),
            _const_spec((1, S5_STATE_ALL)),
            _const_spec((1, D_TOK)),
            _const_spec((D_TOK, D_TOK)),
        ],
        out_specs=pl.BlockSpec((bsz, S5_STEPS, D_TOK), lambda i: (0, i, 0)),
        out_shape=jax.ShapeDtypeStruct((bsz, seq, D_TOK), BF16),
        scratch_shapes=[
            pltpu.VMEM((SUBLANES, S5_STATE_ALL), F32),
            pltpu.VMEM((SUBLANES, S5_STATE_ALL), F32),
            pltpu.VMEM((rows, S5_STATE_ALL), F32),
            pltpu.VMEM((rows, S5_STATE_ALL), F32),
        ],
        compiler_params=pltpu.CompilerParams(dimension_semantics=("arbitrary",),
                                             vmem_limit_bytes=VMEM_LIMIT),
        name="s5_glu",
    )(proj, perm, perm.T, b_bd, c_bd, ab_re, ab_im, d_skip.reshape(1, D_TOK).astype(F32),
      w_glu.astype(BF16))


def _split3(x):
    hi = x.astype(BF16)
    r = x - hi.astype(F32)
    mid = r.astype(BF16)
    lo = (r - mid.astype(F32)).astype(BF16)
    return hi, mid, lo


def _kv_fgate_kernel(x_ref, g_ref, wkv_ref, wf_ref, bf_ref, tri_ref, kv_ref, qb_ref, kb_ref, carry_ref):
    @pl.when(pl.program_id(1) == 0)
    def _():
        carry_ref[...] = jnp.zeros_like(carry_ref)

    hs = _rmsnorm(x_ref[0], g_ref[...]).astype(BF16)
    kv_ref[0] = jnp.dot(hs, wkv_ref[...], preferred_element_type=F32).astype(BF16)
    z = jnp.dot(hs, wf_ref[...], preferred_element_type=F32) + bf_ref[...]
    logf = -(jnp.maximum(-z, 0.0) + jnp.log1p(jnp.exp(-jnp.abs(z))))
    tri = tri_ref[...]
    csum = sum(jnp.dot(tri, part, preferred_element_type=F32) for part in _split3(logf))
    fcum = csum + carry_ref[...]
    carry_ref[...] = fcum[KV_TILE - 1:KV_TILE, :]
    hi, mid, lo = (part.astype(F32) for part in _split3(fcum))
    lane = lax.broadcasted_iota(jnp.int32, fcum.shape, 1)
    j = lane & (GATE_COLS - 1)
    valid = lane < N_FOX_HEADS * GATE_COLS
    one = jnp.ones_like(hi)
    zero = jnp.zeros_like(hi)
    qb = jnp.where(j == 0, hi, jnp.where(j == 1, mid, jnp.where(j == 2, lo, jnp.where(j < 6, one, zero))))
    kb = jnp.where(j < 3, one, jnp.where(j == 3, -hi, jnp.where(j == 4, -mid, jnp.where(j == 5, -lo, zero))))
    qb_ref[0] = jnp.where(valid, qb, zero).astype(BF16)
    kb_ref[0] = jnp.where(valid, kb, zero).astype(BF16)


def _kv_fgate(h, g_kv, w_kv, w_fgate, b_fgate):
    bsz, seq, _ = h.shape
    pad = LANES - N_FOX_HEADS * GATE_COLS
    wf = jnp.pad(jnp.repeat(w_fgate, GATE_COLS, axis=1), ((0, 0), (0, pad))).astype(BF16)
    bfr = jnp.pad(jnp.repeat(b_fgate, GATE_COLS), (0, pad)).reshape(1, LANES).astype(F32)
    tri = (jnp.arange(KV_TILE)[:, None] >= jnp.arange(KV_TILE)[None, :]).astype(BF16)
    return pl.pallas_call(
        _kv_fgate_kernel,
        grid=(bsz, seq // KV_TILE),
        in_specs=[
            pl.BlockSpec((1, KV_TILE, D_MODEL), lambda b, t: (b, t, 0)),
            _const_spec((1, D_MODEL)),
            _const_spec((D_MODEL, 2 * D_TOK)),
            _const_spec((D_MODEL, LANES)),
            _const_spec((1, LANES)),
            _const_spec((KV_TILE, KV_TILE)),
        ],
        out_specs=[
            pl.BlockSpec((1, KV_TILE, 2 * D_TOK), lambda b, t: (b, t, 0)),
            pl.BlockSpec((1, KV_TILE, LANES), lambda b, t: (b, t, 0)),
            pl.BlockSpec((1, KV_TILE, LANES), lambda b, t: (b, t, 0)),
        ],
        out_shape=[
            jax.ShapeDtypeStruct((bsz, seq, 2 * D_TOK), BF16),
            jax.ShapeDtypeStruct((bsz, seq, LANES), BF16),
            jax.ShapeDtypeStruct((bsz, seq, LANES), BF16),
        ],
        scratch_shapes=[pltpu.VMEM((1, LANES), F32)],
        compiler_params=pltpu.CompilerParams(dimension_semantics=("arbitrary", "arbitrary")),
        name="kv_fgate",
    )(h, g_kv.reshape(1, D_MODEL), w_kv.astype(BF16), wf, bfr, tri)


def _fox_kernel(q_ref, qb_ref, k_ref, v_ref, kb_ref, o_ref):
    hp = pl.program_id(1)
    seq = q_ref.shape[1]
    lane = lax.broadcasted_iota(jnp.int32, (FOX_TQ, LANES), 1)
    first = lane < HEAD_DIM
    gate_head = lane // GATE_COLS
    row = lax.broadcasted_iota(jnp.int32, (2 * FOX_TQ, FOX_TK), 0) & (FOX_TQ - 1)
    col = lax.broadcasted_iota(jnp.int32, (2 * FOX_TQ, FOX_TK), 1)

    def q_tile(i, carry):
        r0 = pl.multiple_of(i * FOX_TQ, FOX_TQ)
        q2 = q_ref[0, pl.ds(r0, FOX_TQ), :].astype(F32) * (HEAD_DIM ** -0.5)
        qb = qb_ref[0, pl.ds(r0, FOX_TQ), :].astype(F32)
        zero = jnp.zeros_like(q2)
        qa = jnp.concatenate([jnp.where(first, q2, zero), jnp.where(gate_head == 2 * hp, qb, zero)], axis=1)
        qo = jnp.concatenate([jnp.where(first, zero, q2), jnp.where(gate_head == 2 * hp + 1, qb, zero)], axis=1)
        qq = jnp.concatenate([qa, qo], axis=0).astype(BF16)

        def kv_step(j, state, masked):
            m, l, acc = state
            c0 = pl.multiple_of(j * FOX_TK, FOX_TK)
            kk = jnp.concatenate([k_ref[0, pl.ds(c0, FOX_TK), :], kb_ref[0, pl.ds(c0, FOX_TK), :]], axis=1)
            s = lax.dot_general(qq, kk, (((1,), (1,)), ((), ())), preferred_element_type=F32)
            if masked:
                s = jnp.where(r0 + row >= c0 + col, s, -jnp.inf)
            m_new = jnp.maximum(m, jnp.max(s, axis=1, keepdims=True))
            alpha = jnp.exp(m - m_new)
            p = jnp.exp(s - m_new)
            l_new = alpha * l + jnp.sum(p, axis=1, keepdims=True)
            pv = jnp.dot(p.astype(BF16), v_ref[0, pl.ds(c0, FOX_TK), :], preferred_element_type=F32)
            return m_new, l_new, alpha * acc + pv

        n_full = (i * FOX_TQ) // FOX_TK
        state = (jnp.full((2 * FOX_TQ, 1), -jnp.inf, F32), jnp.zeros((2 * FOX_TQ, 1), F32),
                 jnp.zeros((2 * FOX_TQ, LANES), F32))
        state = lax.fori_loop(0, n_full, lambda j, st: kv_step(j, st, False), state)
        _, l, acc = kv_step(n_full, state, True)
        o = acc / l
        o_ref[0, pl.ds(r0, FOX_TQ), :] = jnp.where(first, o[:FOX_TQ], o[FOX_TQ:]).astype(BF16)
        return carry

    lax.fori_loop(0, seq // FOX_TQ, q_tile, 0)


def _fox(proj, kv, qb, kb):
    bsz, seq, _ = proj.shape
    n_pairs = D_TOK // LANES
    return pl.pallas_call(
        _fox_kernel,
        grid=(bsz, n_pairs),
        in_specs=[
            pl.BlockSpec((1, seq, LANES), lambda b, p: (b, 0, p)),
            pl.BlockSpec((1, seq, LANES), lambda b, p: (b, 0, 0)),
            pl.BlockSpec((1, seq, LANES), lambda b, p: (b, 0, p)),
            pl.BlockSpec((1, seq, LANES), lambda b, p: (b, 0, n_pairs + p)),
            pl.BlockSpec((1, seq, LANES), lambda b, p: (b, 0, 0)),
        ],
        out_specs=pl.BlockSpec((1, seq, LANES), lambda b, p: (b, 0, p)),
        out_shape=jax.ShapeDtypeStruct((bsz, seq, D_TOK), BF16),
        compiler_params=pltpu.CompilerParams(dimension_semantics=("arbitrary", "arbitrary"),
                                             vmem_limit_bytes=VMEM_LIMIT),
        name="fox",
    )(proj, qb, kv, kv, kb)


def _mix_ffn_kernel(h_ref, tok_ref, qm_ref, mkv_ref, wout_ref, gffn_ref, wup_ref, cw_ref, wdn_ref,
                    gfin_ref, o_ref, carry_ref, acc_ref, *, final_norm):
    tm = h_ref.shape[1]

    qm = qm_ref[0].astype(F32) * ((D_MEM // N_MEM_HEADS) ** -0.5)
    km = mkv_ref[0, :, :D_MEM]
    vm = mkv_ref[0, :, D_MEM:]
    lane = lax.broadcasted_iota(jnp.int32, (tm, D_MEM), 1)
    mem = jnp.zeros((tm, D_MEM), F32)
    for hh in range(N_MEM_HEADS):
        sel = (lane >= hh * HEAD_DIM) & (lane < (hh + 1) * HEAD_DIM)
        qh = jnp.where(sel, qm, jnp.zeros_like(qm)).astype(BF16)
        s = lax.dot_general(qh, km, (((1,), (1,)), ((), ())), preferred_element_type=F32)
        p = jnp.exp(s - jnp.max(s, axis=1, keepdims=True))
        o = jnp.dot(p.astype(BF16), vm, preferred_element_type=F32)
        mem = jnp.where(sel, o / jnp.sum(p, axis=1, keepdims=True), mem)

    cat = jnp.concatenate([tok_ref[0], mem.astype(BF16)], axis=1)
    hmid = h_ref[0] + jnp.dot(cat, wout_ref[...], preferred_element_type=F32)
    acc_ref[...] = hmid
    xn = _rmsnorm(hmid, gffn_ref[...]).astype(BF16)

    @pl.when(pl.program_id(1) == 0)
    def _():
        carry_ref[...] = jnp.zeros_like(carry_ref)

    row8 = lax.broadcasted_iota(jnp.int32, (SUBLANES, FF_CHUNK), 0)
    for c in range(N_FF_CHUNKS):
        up = jnp.dot(xn, wup_ref[c], preferred_element_type=F32)
        a = up[:, :FF_CHUNK]
        g = up[:, FF_CHUNK:]
        prev = carry_ref[c]
        carry_ref[c] = g[tm - SUBLANES:, :]
        g1 = pltpu.roll(g, 1, 0)
        g2 = pltpu.roll(g, 2, 0)
        g1 = jnp.concatenate([jnp.where(row8 < 1, pltpu.roll(prev, 1, 0), g1[:SUBLANES]), g1[SUBLANES:]], axis=0)
        g2 = jnp.concatenate([jnp.where(row8 < 2, pltpu.roll(prev, 2, 0), g2[:SUBLANES]), g2[SUBLANES:]], axis=0)
        cw = cw_ref[c]
        gc = g2 * cw[0:1] + g1 * cw[1:2] + g * cw[2:3] + cw[3:4]
        act = (gc * jax.nn.sigmoid(gc) * a).astype(BF16)
        acc_ref[...] += jnp.dot(act, wdn_ref[c], preferred_element_type=F32)

    out = acc_ref[...]
    if final_norm:
        out = _rmsnorm(out, gfin_ref[...])
    o_ref[0] = out


def _mix_ffn(h, tok, proj, memkv, w_out, g_ffn, w_up, conv_w, conv_b, w_down, g_final, final_norm):
    bsz, seq, _ = h.shape
    tm = ROW_TILE
    wa = w_up[:, :D_FF].reshape(D_MODEL, N_FF_CHUNKS, FF_CHUNK)
    wg = w_up[:, D_FF:].reshape(D_MODEL, N_FF_CHUNKS, FF_CHUNK)
    wup = jnp.concatenate([wa, wg], axis=2).transpose(1, 0, 2).astype(BF16)
    wdn = w_down.reshape(N_FF_CHUNKS, FF_CHUNK, D_MODEL).astype(BF16)
    cw = jnp.concatenate([conv_w, conv_b[None, :], jnp.zeros((SUBLANES - 4, D_FF), conv_w.dtype)], axis=0)
    cw = cw.reshape(SUBLANES, N_FF_CHUNKS, FF_CHUNK).transpose(1, 0, 2).astype(F32)
    return pl.pallas_call(
        functools.partial(_mix_ffn_kernel, final_norm=final_norm),
        grid=(bsz, seq // tm),
        in_specs=[
            pl.BlockSpec((1, tm, D_MODEL), lambda b, t: (b, t, 0)),
            pl.BlockSpec((1, tm, D_TOK), lambda b, t: (b, t, 0)),
            pl.BlockSpec((1, tm, D_MEM), lambda b, t: (b, t, D_TOK // D_MEM)),
            pl.BlockSpec((1, MEM_TOKENS, 2 * D_MEM), lambda b, t: (b, 0, 0)),
            _const_spec((D_MODEL, D_MODEL)),
            _const_spec((1, D_MODEL)),
            _const_spec((N_FF_CHUNKS, D_MODEL, 2 * FF_CHUNK)),
            _const_spec((N_FF_CHUNKS, SUBLANES, FF_CHUNK)),
            _const_spec((N_FF_CHUNKS, FF_CHUNK, D_MODEL)),
            _const_spec((1, D_MODEL)),
        ],
        out_specs=pl.BlockSpec((1, tm, D_MODEL), lambda b, t: (b, t, 0)),
        out_shape=jax.ShapeDtypeStruct((bsz, seq, D_MODEL), F32),
        scratch_shapes=[
            pltpu.VMEM((N_FF_CHUNKS, SUBLANES, FF_CHUNK), F32),
            pltpu.VMEM((tm, D_MODEL), F32),
        ],
        compiler_params=pltpu.CompilerParams(dimension_semantics=("arbitrary", "arbitrary"),
                                             vmem_limit_bytes=VMEM_LIMIT),
        name="mix_ffn",
    )(h, tok, proj, memkv, w_out.astype(BF16), g_ffn.reshape(1, D_MODEL), wup, cw, wdn,
      g_final.reshape(1, D_MODEL))


def kernel(x, mem, g_mix, w_in, w_out, g_mem, w_mem_kv, s5_a_re, s5_a_im, s5_log_dt, s5_b_re, s5_b_im,
           s5_c_re, s5_c_im, s5_d, w_glu, g_kv, w_kv, w_fgate, b_fgate, g_ffn, w_ffn_up, conv_w, conv_b,
           w_ffn_down, g_final):
    bsz, seq, _ = x.shape
    depth = w_in.shape[0]
    n_a = depth // 2
    memkv = _memkv(mem, g_mem, w_mem_kv)
    h = x
    kv = qb = kb = None
    for l in range(depth):
        proj = _rms_proj(h.reshape(bsz * seq, D_MODEL), g_mix[l], w_in[l]).reshape(bsz, seq, D_MODEL)
        if l < n_a:
            tok = _s5_glu(proj, s5_a_re[l], s5_a_im[l], s5_log_dt[l], s5_b_re[l], s5_b_im[l],
                          s5_c_re[l], s5_c_im[l], s5_d[l], w_glu[l])
        else:
            tok = _fox(proj, kv, qb, kb)
        h = _mix_ffn(h, tok, proj, memkv[l], w_out[l], g_ffn[l], w_ffn_up[l], conv_w[l], conv_b[l],
                     w_ffn_down[l], g_final, final_norm=(l == depth - 1))
        if l == n_a - 1:
            kv, qb, kb = _kv_fgate(h, g_kv, w_kv, w_fgate, b_fgate)
    return h
```

```python
import functools
import math

import jax
import jax.numpy as jnp
from jax import lax
from jax.experimental import pallas as pl
from jax.experimental.pallas import tpu as pltpu

F32 = jnp.float32
BF16 = jnp.bfloat16

D_MODEL = 1024
HEAD_DIM = 64
D_MEM = 256
N_MEM_HEADS = 4
MEM_TOKENS = 256
D_TOK = D_MODEL - D_MEM
S5_GROUP = 16
S5_GROUPS = D_TOK // S5_GROUP
S5_STATE = 64
N_FOX_HEADS = D_TOK // HEAD_DIM
D_FF = 2816
EPS = 1e-6
LOG2E = math.log2(math.e)

LANES = 128
SUBLANES = 8
MXU_DIM = 256

ROW_TILE = 512
S5_STEPS = MXU_DIM // SUBLANES
S5_SLABS = D_TOK // MXU_DIM
S5_SLAB_STATE = (MXU_DIM // S5_GROUP) * S5_STATE
S5_STATE_ALL = S5_GROUPS * S5_STATE
SCAN_LANES = 512
FF_CHUNK = 256
N_FF_CHUNKS = D_FF // FF_CHUNK
FOX_TQ = 256
FOX_TK = 512
GATE_COLS = 8
KV_TILE = 256
VMEM_LIMIT = 56 * 1024 * 1024


def _rmsnorm(x, g):
    ms = jnp.mean(x * x, axis=-1, keepdims=True)
    return x * lax.rsqrt(ms + EPS) * g


def _const_spec(shape):
    zeros = (0,) * len(shape)
    return pl.BlockSpec(shape, lambda *_: zeros, pipeline_mode=pl.Buffered(1))


def _memkv_kernel(mem_ref, g_ref, w_ref, o_ref):
    mn = _rmsnorm(mem_ref[0], g_ref[...]).astype(BF16)
    for l in range(w_ref.shape[0]):
        o_ref[l, 0] = jnp.dot(mn, w_ref[l], preferred_element_type=F32).astype(BF16)


def _memkv(mem, g_mem, w_mem_kv):
    bsz = mem.shape[0]
    depth = w_mem_kv.shape[0]
    return pl.pallas_call(
        _memkv_kernel,
        grid=(bsz,),
        in_specs=[
            pl.BlockSpec((1, MEM_TOKENS, D_MODEL), lambda b: (b, 0, 0)),
            _const_spec((1, D_MODEL)),
            _const_spec((depth, D_MODEL, 2 * D_MEM)),
        ],
        out_specs=pl.BlockSpec((depth, 1, MEM_TOKENS, 2 * D_MEM), lambda b: (0, b, 0, 0)),
        out_shape=jax.ShapeDtypeStruct((depth, bsz, MEM_TOKENS, 2 * D_MEM), BF16),
        compiler_params=pltpu.CompilerParams(dimension_semantics=("arbitrary",)),
        name="memkv",
    )(mem, g_mem.reshape(1, D_MODEL), w_mem_kv.astype(BF16))


def _rms_proj_kernel(x_ref, g_ref, w_ref, o_ref):
    xn = _rmsnorm(x_ref[...], g_ref[...]).astype(BF16)
    o_ref[...] = jnp.dot(xn, w_ref[...], preferred_element_type=F32).astype(o_ref.dtype)


def _rms_proj(h2d, g, w):
    rows = h2d.shape[0]
    n = w.shape[1]
    return pl.pallas_call(
        _rms_proj_kernel,
        grid=(rows // ROW_TILE,),
        in_specs=[
            pl.BlockSpec((ROW_TILE, D_MODEL), lambda i: (i, 0)),
            _const_spec((1, D_MODEL)),
            _const_spec((D_MODEL, n)),
        ],
        out_specs=pl.BlockSpec((ROW_TILE, n), lambda i: (i, 0)),
        out_shape=jax.ShapeDtypeStruct((rows, n), BF16),
        compiler_params=pltpu.CompilerParams(dimension_semantics=("arbitrary",)),
        name="rms_proj",
    )(h2d, g.reshape(1, D_MODEL), w.astype(BF16))


def _s5_discretise(a_re, a_im, log_dt, b_re, b_im, c_re, c_im):
    dt = jnp.exp(log_dt.astype(F32))[:, None]
    lam_re = jnp.minimum(a_re.astype(F32), -1e-4)
    lam_im = a_im.astype(F32)
    mag = jnp.exp(lam_re * dt)
    ph = lam_im * dt
    ab_re, ab_im = mag * jnp.cos(ph), mag * jnp.sin(ph)
    den = lam_re * lam_re + lam_im * lam_im
    z_re = ((ab_re - 1.0) * lam_re + ab_im * lam_im) / den
    z_im = (ab_im * lam_re - (ab_re - 1.0) * lam_im) / den
    br, bi = b_re.astype(F32), b_im.astype(F32)
    bb_re = z_re[..., None] * br - z_im[..., None] * bi
    bb_im = z_re[..., None] * bi + z_im[..., None] * br
    gps = MXU_DIM // S5_GROUP
    eye = jnp.eye(gps, dtype=F32)

    def pack_b(bb):
        bb = bb.reshape(S5_SLABS, gps, S5_STATE, S5_GROUP)
        return jnp.einsum('sgpi,gh->sgihp', bb, eye).reshape(S5_SLABS, MXU_DIM, S5_SLAB_STATE)

    def pack_c(cc):
        cc = cc.reshape(S5_SLABS, gps, S5_GROUP, S5_STATE)
        return jnp.einsum('sgip,gh->sgphi', cc, eye).reshape(S5_SLABS, S5_SLAB_STATE, MXU_DIM)

    b_bd = jnp.concatenate([pack_b(bb_re), pack_b(bb_im)], axis=2).astype(BF16)
    c_bd = jnp.concatenate([pack_c(c_re.astype(F32)), -pack_c(c_im.astype(F32))], axis=1).astype(BF16)
    return ab_re.reshape(1, S5_STATE_ALL), ab_im.reshape(1, S5_STATE_ALL), b_bd, c_bd


def _s5_glu_kernel(u_ref, perm_ref, permt_ref, bbd_ref, cbd_ref, are_ref, aim_ref, d_ref, wglu_ref,
                   o_ref, hre_ref, him_ref, sre_ref, sim_ref):
    rows = SUBLANES * S5_STEPS

    @pl.when(pl.program_id(0) == 0)
    def _():
        hre_ref[...] = jnp.zeros_like(hre_ref)
        him_ref[...] = jnp.zeros_like(him_ref)

    u = u_ref[...].reshape(rows, D_TOK)
    up = jnp.dot(perm_ref[...], u, preferred_element_type=F32).astype(BF16)

    for s in range(S5_SLABS):
        bu = jnp.dot(up[:, s * MXU_DIM:(s + 1) * MXU_DIM], bbd_ref[s], preferred_element_type=F32)
        sre_ref[:, s * S5_SLAB_STATE:(s + 1) * S5_SLAB_STATE] = bu[:, :S5_SLAB_STATE]
        sim_ref[:, s * S5_SLAB_STATE:(s + 1) * S5_SLAB_STATE] = bu[:, S5_SLAB_STATE:]

    for c in range(S5_STATE_ALL // SCAN_LANES):
        cs = slice(c * SCAN_LANES, (c + 1) * SCAN_LANES)
        ar = jnp.broadcast_to(are_ref[:, cs], (SUBLANES, SCAN_LANES))
        ai = jnp.broadcast_to(aim_ref[:, cs], (SUBLANES, SCAN_LANES))
        hr = hre_ref[:, cs]
        hi = him_ref[:, cs]
        for t in range(S5_STEPS):
            rs = slice(t * SUBLANES, (t + 1) * SUBLANES)
            br = sre_ref[rs, cs]
            bi = sim_ref[rs, cs]
            hr, hi = ar * hr - ai * hi + br, ar * hi + ai * hr + bi
            sre_ref[rs, cs] = hr
            sim_ref[rs, cs] = hi
        hre_ref[:, cs] = hr
        him_ref[:, cs] = hi

    ys = []
    for s in range(S5_SLABS):
        ss = slice(s * S5_SLAB_STATE, (s + 1) * S5_SLAB_STATE)
        hcat = jnp.concatenate([sre_ref[:, ss], sim_ref[:, ss]], axis=1).astype(BF16)
        ys.append(jnp.dot(hcat, cbd_ref[s], preferred_element_type=F32))
    y = jnp.concatenate(ys, axis=1) + d_ref[...] * up.astype(F32)
    g = jax.nn.gelu(y)
    z = jnp.dot(g.astype(BF16), wglu_ref[...], preferred_element_type=F32)
    tok = (g * jax.nn.sigmoid(z)).astype(BF16)
    out = jnp.dot(permt_ref[...], tok, preferred_element_type=F32).astype(BF16)
    o_ref[...] = out.reshape(SUBLANES, S5_STEPS, D_TOK)


def _s5_glu(proj, a_re, a_im, log_dt, b_re, b_im, c_re, c_im, d_skip, w_glu):
    bsz, seq, _ = proj.shape
    assert bsz == SUBLANES, "the scan keeps one batch element per sublane"
    rows = SUBLANES * S5_STEPS
    ab_re, ab_im, b_bd, c_bd = _s5_discretise(a_re, a_im, log_dt, b_re, b_im, c_re, c_im)
    src = (jnp.arange(rows) % SUBLANES) * S5_STEPS + jnp.arange(rows) // SUBLANES
    perm = (src[:, None] == jnp.arange(rows)[None, :]).astype(BF16)
    return pl.pallas_call(
        _s5_glu_kernel,
        grid=(seq // S5_STEPS,),
        in_specs=[
            pl.BlockSpec((bsz, S5_STEPS, D_TOK), lambda i: (0, i, 0)),
            _const_spec((rows, rows)),
            _const_spec((rows, rows)),
            _const_spec((S5_SLABS, MXU_DIM, 2 * S5_SLAB_STATE)),
            _const_spec((S5_SLABS, 2 * S5_SLAB_STATE, MXU_DIM)),
            _const_spec((1, S5_STATE_ALL)),
            _const_spec((1, S5_STATE_ALL)),
            _const_spec((1, D_TOK)),
            _const_spec((D_TOK, D_TOK)),
        ],
        out_specs=pl.BlockSpec((bsz, S5_STEPS, D_TOK), lambda i: (0, i, 0)),
        out_shape=jax.ShapeDtypeStruct((bsz, seq, D_TOK), BF16),
        scratch_shapes=[
            pltpu.VMEM((SUBLANES, S5_STATE_ALL), F32),
            pltpu.VMEM((SUBLANES, S5_STATE_ALL), F32),
            pltpu.VMEM((rows, S5_STATE_ALL), F32),
            pltpu.VMEM((rows, S5_STATE_ALL), F32),
        ],
        compiler_params=pltpu.CompilerParams(dimension_semantics=("arbitrary",),
                                             vmem_limit_bytes=VMEM_LIMIT),
        name="s5_glu",
    )(proj, perm, perm.T, b_bd, c_bd, ab_re, ab_im, d_skip.reshape(1, D_TOK).astype(F32),
      w_glu.astype(BF16))


def _split3(x):
    hi = x.astype(BF16)
    r = x - hi.astype(F32)
    mid = r.astype(BF16)
    lo = (r - mid.astype(F32)).astype(BF16)
    return hi, mid, lo


def _kv_fgate_kernel(x_ref, g_ref, wkv_ref, wf_ref, bf_ref, tri_ref, kv_ref, qb_ref, kb_ref, carry_ref):
    @pl.when(pl.program_id(1) == 0)
    def _():
        carry_ref[...] = jnp.zeros_like(carry_ref)

    hs = _rmsnorm(x_ref[0], g_ref[...]).astype(BF16)
    kv_ref[0] = jnp.dot(hs, wkv_ref[...], preferred_element_type=F32).astype(BF16)
    z = jnp.dot(hs, wf_ref[...], preferred_element_type=F32) + bf_ref[...]
    logf = -(jnp.maximum(-z, 0.0) + jnp.log1p(jnp.exp(-jnp.abs(z))))
    tri = tri_ref[...]
    csum = sum(jnp.dot(tri, part, preferred_element_type=F32) for part in _split3(logf))
    fcum = csum + carry_ref[...]
    carry_ref[...] = fcum[KV_TILE - 1:KV_TILE, :]
    hi, mid, lo = (part.astype(F32) for part in _split3(fcum * LOG2E))
    lane = lax.broadcasted_iota(jnp.int32, fcum.shape, 1)
    j = lane & (GATE_COLS - 1)
    valid = lane < N_FOX_HEADS * GATE_COLS
    one = jnp.ones_like(hi)
    zero = jnp.zeros_like(hi)
    qb = jnp.where(j == 0, hi, jnp.where(j == 1, mid, jnp.where(j == 2, lo, jnp.where(j < 6, one, zero))))
    kb = jnp.where(j < 3, one, jnp.where(j == 3, -hi, jnp.where(j == 4, -mid, jnp.where(j == 5, -lo, zero))))
    qb_ref[0] = jnp.where(valid, qb, zero).astype(BF16)
    kb_ref[0] = jnp.where(valid, kb, zero).astype(BF16)


def _kv_fgate(h, g_kv, w_kv, w_fgate, b_fgate):
    bsz, seq, _ = h.shape
    pad = LANES - N_FOX_HEADS * GATE_COLS
    wf = jnp.pad(jnp.repeat(w_fgate, GATE_COLS, axis=1), ((0, 0), (0, pad))).astype(BF16)
    bfr = jnp.pad(jnp.repeat(b_fgate, GATE_COLS), (0, pad)).reshape(1, LANES).astype(F32)
    tri = (jnp.arange(KV_TILE)[:, None] >= jnp.arange(KV_TILE)[None, :]).astype(BF16)
    return pl.pallas_call(
        _kv_fgate_kernel,
        grid=(bsz, seq // KV_TILE),
        in_specs=[
            pl.BlockSpec((1, KV_TILE, D_MODEL), lambda b, t: (b, t, 0)),
            _const_spec((1, D_MODEL)),
            _const_spec((D_MODEL, 2 * D_TOK)),
            _const_spec((D_MODEL, LANES)),
            _const_spec((1, LANES)),
            _const_spec((KV_TILE, KV_TILE)),
        ],
        out_specs=[
            pl.BlockSpec((1, KV_TILE, 2 * D_TOK), lambda b, t: (b, t, 0)),
            pl.BlockSpec((1, KV_TILE, LANES), lambda b, t: (b, t, 0)),
            pl.BlockSpec((1, KV_TILE, LANES), lambda b, t: (b, t, 0)),
        ],
        out_shape=[
            jax.ShapeDtypeStruct((bsz, seq, 2 * D_TOK), BF16),
            jax.ShapeDtypeStruct((bsz, seq, LANES), BF16),
            jax.ShapeDtypeStruct((bsz, seq, LANES), BF16),
        ],
        scratch_shapes=[pltpu.VMEM((1, LANES), F32)],
        compiler_params=pltpu.CompilerParams(dimension_semantics=("arbitrary", "arbitrary")),
        name="kv_fgate",
    )(h, g_kv.reshape(1, D_MODEL), w_kv.astype(BF16), wf, bfr, tri)


def _fox_kernel(itab_ref, jtab_ref, q_ref, qb_ref, k_ref, v_ref, kb_ref, o_ref,
                qq_ref, m_ref, acc_ref, s_ref, p_ref, a_ref, *, n_steps):
    hp = pl.program_id(1)
    seq = q_ref.shape[1]
    n_q = seq // FOX_TQ
    rows = 2 * FOX_TQ
    n_lane_blocks = FOX_TK // LANES
    ones_cols = jnp.ones((FOX_TK, LANES), BF16)

    def scores(i, j):
        c0 = pl.multiple_of(j * FOX_TK, FOX_TK)
        kk = jnp.concatenate([k_ref[0, pl.ds(c0, FOX_TK), :], kb_ref[0, pl.ds(c0, FOX_TK), :]], axis=1)
        return lax.dot_general(qq_ref[i], kk, (((1,), (1,)), ((), ())), preferred_element_type=F32)

    def lane_blocks(s):
        return [s[:, n * LANES:(n + 1) * LANES] for n in range(n_lane_blocks)]

    def row_max(blocks):
        mx = functools.reduce(jnp.maximum, blocks)
        return jnp.broadcast_to(jnp.max(mx, axis=1, keepdims=True), (rows, LANES))

    def probs(blocks, m):
        return jnp.concatenate([jnp.exp2(blk - m).astype(BF16) for blk in blocks], axis=1)

    def pv_dot(p, j):
        c0 = pl.multiple_of(j * FOX_TK, FOX_TK)
        vv = jnp.concatenate([v_ref[0, pl.ds(c0, FOX_TK), :], ones_cols], axis=1)
        return jnp.dot(p, vv, preferred_element_type=F32)

    lane = lax.broadcasted_iota(jnp.int32, (FOX_TQ, LANES), 1)
    first = lane < HEAD_DIM
    gate_head = lane // GATE_COLS

    def build_q(i, carry):
        r0 = pl.multiple_of(i * FOX_TQ, FOX_TQ)
        q2 = q_ref[0, pl.ds(r0, FOX_TQ), :].astype(F32)
        qb = qb_ref[0, pl.ds(r0, FOX_TQ), :].astype(F32)
        zero = jnp.zeros_like(q2)
        qa = jnp.concatenate([jnp.where(first, q2, zero), jnp.where(gate_head == 2 * hp, qb, zero)], axis=1)
        qo = jnp.concatenate([jnp.where(first, zero, q2), jnp.where(gate_head == 2 * hp + 1, qb, zero)], axis=1)
        qq_ref[i] = jnp.concatenate([qa, qo], axis=0).astype(BF16)
        return carry

    lax.fori_loop(0, n_q, build_q, 0)

    row = lax.broadcasted_iota(jnp.int32, (rows, FOX_TK), 0) & (FOX_TQ - 1)
    col = lax.broadcasted_iota(jnp.int32, (rows, FOX_TK), 1)

    def diagonal(i, carry):
        j = (i * FOX_TQ) // FOX_TK
        s = jnp.where(i * FOX_TQ + row >= j * FOX_TK + col, scores(i, j), -jnp.inf)
        blocks = lane_blocks(s)
        m = row_max(blocks)
        m_ref[i] = m
        acc_ref[i] = pv_dot(probs(blocks, m), j)
        return carry

    lax.fori_loop(0, n_q, diagonal, 0)

    def stage_scores(t, slot):
        s_ref[slot] = scores(itab_ref[t], jtab_ref[t])

    def stage_softmax(t, slot):
        i = itab_ref[t]
        blocks = lane_blocks(s_ref[slot])
        m_old = m_ref[i]
        m_new = jnp.maximum(m_old, row_max(blocks))
        m_ref[i] = m_new
        a_ref[slot] = jnp.exp2(m_old - m_new)
        p_ref[slot] = probs(blocks, m_new)

    def stage_pv(t, slot):
        i = itab_ref[t]
        alpha = a_ref[slot]
        acc_ref[i] = jnp.concatenate([alpha, alpha], axis=1) * acc_ref[i] + pv_dot(p_ref[slot], jtab_ref[t])

    assert n_steps % 2 == 0 and n_steps >= 4
    stage_scores(0, 0)
    stage_scores(1, 1)
    stage_softmax(0, 0)

    def steady(u, carry):
        t = 2 * u
        stage_scores(t + 2, 0)
        stage_softmax(t + 1, 1)
        stage_pv(t, 0)
        stage_scores(t + 3, 1)
        stage_softmax(t + 2, 0)
        stage_pv(t + 1, 1)
        return carry

    lax.fori_loop(0, (n_steps - 2) // 2, steady, 0)
    stage_softmax(n_steps - 1, 1)
    stage_pv(n_steps - 2, 0)
    stage_pv(n_steps - 1, 1)

    def finish(i, carry):
        acc = acc_ref[i]
        o = acc[:, :LANES] / acc[:, LANES:]
        r0 = pl.multiple_of(i * FOX_TQ, FOX_TQ)
        o_ref[0, pl.ds(r0, FOX_TQ), :] = jnp.where(first, o[:FOX_TQ], o[FOX_TQ:]).astype(BF16)
        return carry

    lax.fori_loop(0, n_q, finish, 0)


def _fox(proj, kv, qb, kb):
    bsz, seq, _ = proj.shape
    n_pairs = D_TOK // LANES
    n_q = seq // FOX_TQ
    rows = 2 * FOX_TQ
    pairs = [(i, j) for j in range(seq // FOX_TK) for i in range(n_q) if (i * FOX_TQ) // FOX_TK > j]
    itab = jnp.asarray([p[0] for p in pairs], jnp.int32)
    jtab = jnp.asarray([p[1] for p in pairs], jnp.int32)
    grid_spec = pltpu.PrefetchScalarGridSpec(
        num_scalar_prefetch=2,
        grid=(bsz, n_pairs),
        in_specs=[
            pl.BlockSpec((1, seq, LANES), lambda b, p, *_: (b, 0, p)),
            pl.BlockSpec((1, seq, LANES), lambda b, p, *_: (b, 0, 0)),
            pl.BlockSpec((1, seq, LANES), lambda b, p, *_: (b, 0, p)),
            pl.BlockSpec((1, seq, LANES), lambda b, p, *_: (b, 0, n_pairs + p)),
            pl.BlockSpec((1, seq, LANES), lambda b, p, *_: (b, 0, 0)),
        ],
        out_specs=pl.BlockSpec((1, seq, LANES), lambda b, p, *_: (b, 0, p)),
        scratch_shapes=[
            pltpu.VMEM((n_q, rows, 2 * LANES), BF16),
            pltpu.VMEM((n_q, rows, LANES), F32),
            pltpu.VMEM((n_q, rows, 2 * LANES), F32),
            pltpu.VMEM((2, rows, FOX_TK), F32),
            pltpu.VMEM((2, rows, FOX_TK), BF16),
            pltpu.VMEM((2, rows, LANES), F32),
        ],
    )
    return pl.pallas_call(
        functools.partial(_fox_kernel, n_steps=len(pairs)),
        grid_spec=grid_spec,
        out_shape=jax.ShapeDtypeStruct((bsz, seq, D_TOK), BF16),
        compiler_params=pltpu.CompilerParams(dimension_semantics=("arbitrary", "arbitrary"),
                                             vmem_limit_bytes=VMEM_LIMIT),
        name="fox",
    )(itab, jtab, proj, qb, kv, kv, kb)


def _mix_ffn_kernel(h_ref, tok_ref, qm_ref, mkv_ref, wout_ref, gffn_ref, wup_ref, cw_ref, wdn_ref,
                    gfin_ref, o_ref, carry_ref, acc_ref, xn_ref, up_ref, *, final_norm):
    tm = h_ref.shape[1]

    qm = qm_ref[0].astype(F32) * ((D_MEM // N_MEM_HEADS) ** -0.5)
    km = mkv_ref[0, :, :D_MEM]
    vm = mkv_ref[0, :, D_MEM:]
    lane = lax.broadcasted_iota(jnp.int32, (tm, D_MEM), 1)
    mem = jnp.zeros((tm, D_MEM), F32)
    for hh in range(N_MEM_HEADS):
        sel = (lane >= hh * HEAD_DIM) & (lane < (hh + 1) * HEAD_DIM)
        qh = jnp.where(sel, qm, jnp.zeros_like(qm)).astype(BF16)
        s = lax.dot_general(qh, km, (((1,), (1,)), ((), ())), preferred_element_type=F32)
        p = jnp.exp(s - jnp.max(s, axis=1, keepdims=True))
        o = jnp.dot(p.astype(BF16), vm, preferred_element_type=F32)
        mem = jnp.where(sel, o / jnp.sum(p, axis=1, keepdims=True), mem)

    cat = jnp.concatenate([tok_ref[0], mem.astype(BF16)], axis=1)
    hmid = h_ref[0] + jnp.dot(cat, wout_ref[...], preferred_element_type=F32)
    acc_ref[...] = hmid
    xn = _rmsnorm(hmid, gffn_ref[...]).astype(BF16)

    @pl.when(pl.program_id(1) == 0)
    def _():
        carry_ref[...] = jnp.zeros_like(carry_ref)

    xn_ref[...] = xn
    row8 = lax.broadcasted_iota(jnp.int32, (SUBLANES, FF_CHUNK), 0)
    up_ref[0] = jnp.dot(xn_ref[...], wup_ref[0], preferred_element_type=F32)
    for c in range(N_FF_CHUNKS):
        if c + 1 < N_FF_CHUNKS:
            up_ref[(c + 1) % 2] = jnp.dot(xn_ref[...], wup_ref[c + 1], preferred_element_type=F32)
        a = up_ref[c % 2, :, :FF_CHUNK]
        g = up_ref[c % 2, :, FF_CHUNK:]
        prev = carry_ref[c]
        carry_ref[c] = g[tm - SUBLANES:, :]
        g1 = pltpu.roll(g, 1, 0)
        g2 = pltpu.roll(g, 2, 0)
        g1 = jnp.concatenate([jnp.where(row8 < 1, pltpu.roll(prev, 1, 0), g1[:SUBLANES]), g1[SUBLANES:]], axis=0)
        g2 = jnp.concatenate([jnp.where(row8 < 2, pltpu.roll(prev, 2, 0), g2[:SUBLANES]), g2[SUBLANES:]], axis=0)
        cw = cw_ref[c]
        gc = g2 * cw[0:1] + g1 * cw[1:2] + g * cw[2:3] + cw[3:4]
        act = (gc * jax.nn.sigmoid(gc) * a).astype(BF16)
        acc_ref[...] += jnp.dot(act, wdn_ref[c], preferred_element_type=F32)

    out = acc_ref[...]
    if final_norm:
        out = _rmsnorm(out, gfin_ref[...])
    o_ref[0] = out


def _mix_ffn(h, tok, proj, memkv, w_out, g_ffn, w_up, conv_w, conv_b, w_down, g_final, final_norm):
    bsz, seq, _ = h.shape
    tm = ROW_TILE
    wa = w_up[:, :D_FF].reshape(D_MODEL, N_FF_CHUNKS, FF_CHUNK)
    wg = w_up[:, D_FF:].reshape(D_MODEL, N_FF_CHUNKS, FF_CHUNK)
    wup = jnp.concatenate([wa, wg], axis=2).transpose(1, 0, 2).astype(BF16)
    wdn = w_down.reshape(N_FF_CHUNKS, FF_CHUNK, D_MODEL).astype(BF16)
    cw = jnp.concatenate([conv_w, conv_b[None, :], jnp.zeros((SUBLANES - 4, D_FF), conv_w.dtype)], axis=0)
    cw = cw.reshape(SUBLANES, N_FF_CHUNKS, FF_CHUNK).transpose(1, 0, 2).astype(F32)
    return pl.pallas_call(
        functools.partial(_mix_ffn_kernel, final_norm=final_norm),
        grid=(bsz, seq // tm),
        in_specs=[
            pl.BlockSpec((1, tm, D_MODEL), lambda b, t: (b, t, 0)),
            pl.BlockSpec((1, tm, D_TOK), lambda b, t: (b, t, 0)),
            pl.BlockSpec((1, tm, D_MEM), lambda b, t: (b, t, D_TOK // D_MEM)),
            pl.BlockSpec((1, MEM_TOKENS, 2 * D_MEM), lambda b, t: (b, 0, 0)),
            _const_spec((D_MODEL, D_MODEL)),
            _const_spec((1, D_MODEL)),
            _const_spec((N_FF_CHUNKS, D_MODEL, 2 * FF_CHUNK)),
            _const_spec((N_FF_CHUNKS, SUBLANES, FF_CHUNK)),
            _const_spec((N_FF_CHUNKS, FF_CHUNK, D_MODEL)),
            _const_spec((1, D_MODEL)),
        ],
        out_specs=pl.BlockSpec((1, tm, D_MODEL), lambda b, t: (b, t, 0)),
        out_shape=jax.ShapeDtypeStruct((bsz, seq, D_MODEL), F32),
        scratch_shapes=[
            pltpu.VMEM((N_FF_CHUNKS, SUBLANES, FF_CHUNK), F32),
            pltpu.VMEM((tm, D_MODEL), F32),
            pltpu.VMEM((tm, D_MODEL), BF16),
            pltpu.VMEM((2, tm, 2 * FF_CHUNK), F32),
        ],
        compiler_params=pltpu.CompilerParams(dimension_semantics=("arbitrary", "arbitrary"),
                                             vmem_limit_bytes=VMEM_LIMIT),
        name="mix_ffn",
    )(h, tok, proj, memkv, w_out.astype(BF16), g_ffn.reshape(1, D_MODEL), wup, cw, wdn,
      g_final.reshape(1, D_MODEL))


def kernel(x, mem, g_mix, w_in, w_out, g_mem, w_mem_kv, s5_a_re, s5_a_im, s5_log_dt, s5_b_re, s5_b_im,
           s5_c_re, s5_c_im, s5_d, w_glu, g_kv, w_kv, w_fgate, b_fgate, g_ffn, w_ffn_up, conv_w, conv_b,
           w_ffn_down, g_final):
    bsz, seq, _ = x.shape
    depth = w_in.shape[0]
    n_a = depth // 2
    memkv = _memkv(mem, g_mem, w_mem_kv)
    h = x
    kv = qb = kb = None
    for l in range(depth):
        w_l = w_in[l]
        if l >= n_a:
            q_scale = jnp.where(jnp.arange(D_MODEL) < D_TOK, (HEAD_DIM ** -0.5) * LOG2E, 1.0).astype(F32)
            w_l = w_l * q_scale[None, :]
        proj = _rms_proj(h.reshape(bsz * seq, D_MODEL), g_mix[l], w_l).reshape(bsz, seq, D_MODEL)
        if l < n_a:
            tok = _s5_glu(proj, s5_a_re[l], s5_a_im[l], s5_log_dt[l], s5_b_re[l], s5_b_im[l],
                          s5_c_re[l], s5_c_im[l], s5_d[l], w_glu[l])
        else:
            tok = _fox(proj, kv, qb, kb)
        h = _mix_ffn(h, tok, proj, memkv[l], w_out[l], g_ffn[l], w_ffn_up[l], conv_w[l], conv_b[l],
                     w_ffn_down[l], g_final, final_norm=(l == depth - 1))
        if l == n_a - 1:
            kv, qb, kb = _kv_fgate(h, g_kv, w_kv, w_fgate, b_fgate)
    return h
```

```python
import functools
import math

import jax
import jax.numpy as jnp
from jax import lax
from jax.experimental import pallas as pl
from jax.experimental.pallas import tpu as pltpu

F32 = jnp.float32
BF16 = jnp.bfloat16

D_MODEL = 1024
HEAD_DIM = 64
D_MEM = 256
N_MEM_HEADS = 4
MEM_TOKENS = 256
D_TOK = D_MODEL - D_MEM
S5_GROUP = 16
S5_GROUPS = D_TOK // S5_GROUP
S5_STATE = 64
N_FOX_HEADS = D_TOK // HEAD_DIM
D_FF = 2816
EPS = 1e-6
LOG2E = math.log2(math.e)

LANES = 128
SUBLANES = 8
MXU_DIM = 256

ROW_TILE = 512
S5_STEPS = MXU_DIM // SUBLANES
S5_SLABS = D_TOK // MXU_DIM
S5_SLAB_STATE = (MXU_DIM // S5_GROUP) * S5_STATE
S5_STATE_ALL = S5_GROUPS * S5_STATE
SCAN_LANES = 512
FF_CHUNK = 256
N_FF_CHUNKS = D_FF // FF_CHUNK
FOX_TQ = 256
FOX_TK = 512
GATE_COLS = 8
KV_TILE = 256
VMEM_LIMIT = 56 * 1024 * 1024


def _rmsnorm(x, g):
    ms = jnp.mean(x * x, axis=-1, keepdims=True)
    return x * lax.rsqrt(ms + EPS) * g


def _const_spec(shape):
    zeros = (0,) * len(shape)
    return pl.BlockSpec(shape, lambda *_: zeros, pipeline_mode=pl.Buffered(1))


def _memkv_kernel(mem_ref, g_ref, w_ref, o_ref):
    mn = _rmsnorm(mem_ref[0], g_ref[...]).astype(BF16)
    for l in range(w_ref.shape[0]):
        o_ref[l, 0] = jnp.dot(mn, w_ref[l], preferred_element_type=F32).astype(BF16)


def _memkv(mem, g_mem, w_mem_kv):
    bsz = mem.shape[0]
    depth = w_mem_kv.shape[0]
    return pl.pallas_call(
        _memkv_kernel,
        grid=(bsz,),
        in_specs=[
            pl.BlockSpec((1, MEM_TOKENS, D_MODEL), lambda b: (b, 0, 0)),
            _const_spec((1, D_MODEL)),
            _const_spec((depth, D_MODEL, 2 * D_MEM)),
        ],
        out_specs=pl.BlockSpec((depth, 1, MEM_TOKENS, 2 * D_MEM), lambda b: (0, b, 0, 0)),
        out_shape=jax.ShapeDtypeStruct((depth, bsz, MEM_TOKENS, 2 * D_MEM), BF16),
        compiler_params=pltpu.CompilerParams(dimension_semantics=("arbitrary",)),
        name="memkv",
    )(mem, g_mem.reshape(1, D_MODEL), w_mem_kv.astype(BF16))


def _rms_proj_kernel(x_ref, g_ref, w_ref, o_ref):
    xn = _rmsnorm(x_ref[...], g_ref[...]).astype(BF16)
    o_ref[...] = jnp.dot(xn, w_ref[...], preferred_element_type=F32).astype(o_ref.dtype)


def _rms_proj(h2d, g, w):
    rows = h2d.shape[0]
    n = w.shape[1]
    return pl.pallas_call(
        _rms_proj_kernel,
        grid=(rows // ROW_TILE,),
        in_specs=[
            pl.BlockSpec((ROW_TILE, D_MODEL), lambda i: (i, 0)),
            _const_spec((1, D_MODEL)),
            _const_spec((D_MODEL, n)),
        ],
        out_specs=pl.BlockSpec((ROW_TILE, n), lambda i: (i, 0)),
        out_shape=jax.ShapeDtypeStruct((rows, n), BF16),
        compiler_params=pltpu.CompilerParams(dimension_semantics=("arbitrary",)),
        name="rms_proj",
    )(h2d, g.reshape(1, D_MODEL), w.astype(BF16))


def _s5_discretise(a_re, a_im, log_dt, b_re, b_im, c_re, c_im):
    dt = jnp.exp(log_dt.astype(F32))[:, None]
    lam_re = jnp.minimum(a_re.astype(F32), -1e-4)
    lam_im = a_im.astype(F32)
    mag = jnp.exp(lam_re * dt)
    ph = lam_im * dt
    ab_re, ab_im = mag * jnp.cos(ph), mag * jnp.sin(ph)
    den = lam_re * lam_re + lam_im * lam_im
    z_re = ((ab_re - 1.0) * lam_re + ab_im * lam_im) / den
    z_im = (ab_im * lam_re - (ab_re - 1.0) * lam_im) / den
    br, bi = b_re.astype(F32), b_im.astype(F32)
    bb_re = z_re[..., None] * br - z_im[..., None] * bi
    bb_im = z_re[..., None] * bi + z_im[..., None] * br
    gps = MXU_DIM // S5_GROUP
    eye = jnp.eye(gps, dtype=F32)

    def pack_b(bb):
        bb = bb.reshape(S5_SLABS, gps, S5_STATE, S5_GROUP)
        return jnp.einsum('sgpi,gh->sgihp', bb, eye).reshape(S5_SLABS, MXU_DIM, S5_SLAB_STATE)

    def pack_c(cc):
        cc = cc.reshape(S5_SLABS, gps, S5_GROUP, S5_STATE)
        return jnp.einsum('sgip,gh->sgphi', cc, eye).reshape(S5_SLABS, S5_SLAB_STATE, MXU_DIM)

    b_bd = jnp.concatenate([pack_b(bb_re), pack_b(bb_im)], axis=2).astype(BF16)
    c_bd = jnp.concatenate([pack_c(c_re.astype(F32)), -pack_c(c_im.astype(F32))], axis=1).astype(BF16)
    return ab_re.reshape(1, S5_STATE_ALL), ab_im.reshape(1, S5_STATE_ALL), b_bd, c_bd


def _s5_glu_kernel(u_ref, perm_ref, permt_ref, bbd_ref, cbd_ref, are_ref, aim_ref, d_ref, wglu_ref,
                   o_ref, hre_ref, him_ref, sre_ref, sim_ref):
    rows = SUBLANES * S5_STEPS

    @pl.when(pl.program_id(0) == 0)
    def _():
        hre_ref[...] = jnp.zeros_like(hre_ref)
        him_ref[...] = jnp.zeros_like(him_ref)

    u = u_ref[...].reshape(rows, D_TOK)
    up = jnp.dot(perm_ref[...], u, preferred_element_type=F32).astype(BF16)

    for s in range(S5_SLABS):
        bu = jnp.dot(up[:, s * MXU_DIM:(s + 1) * MXU_DIM], bbd_ref[s], preferred_element_type=F32)
        sre_ref[:, s * S5_SLAB_STATE:(s + 1) * S5_SLAB_STATE] = bu[:, :S5_SLAB_STATE]
        sim_ref[:, s * S5_SLAB_STATE:(s + 1) * S5_SLAB_STATE] = bu[:, S5_SLAB_STATE:]

    for c in range(S5_STATE_ALL // SCAN_LANES):
        cs = slice(c * SCAN_LANES, (c + 1) * SCAN_LANES)
        ar = jnp.broadcast_to(are_ref[:, cs], (SUBLANES, SCAN_LANES))
        ai = jnp.broadcast_to(aim_ref[:, cs], (SUBLANES, SCAN_LANES))
        hr = hre_ref[:, cs]
        hi = him_ref[:, cs]
        for t in range(S5_STEPS):
            rs = slice(t * SUBLANES, (t + 1) * SUBLANES)
            br = sre_ref[rs, cs]
            bi = sim_ref[rs, cs]
            hr, hi = ar * hr - ai * hi + br, ar * hi + ai * hr + bi
            sre_ref[rs, cs] = hr
            sim_ref[rs, cs] = hi
        hre_ref[:, cs] = hr
        him_ref[:, cs] = hi

    ys = []
    for s in range(S5_SLABS):
        ss = slice(s * S5_SLAB_STATE, (s + 1) * S5_SLAB_STATE)
        hcat = jnp.concatenate([sre_ref[:, ss], sim_ref[:, ss]], axis=1).astype(BF16)
        ys.append(jnp.dot(hcat, cbd_ref[s], preferred_element_type=F32))
    y = jnp.concatenate(ys, axis=1) + d_ref[...] * up.astype(F32)
    g = jax.nn.gelu(y)
    z = jnp.dot(g.astype(BF16), wglu_ref[...], preferred_element_type=F32)
    tok = (g * jax.nn.sigmoid(z)).astype(BF16)
    out = jnp.dot(permt_ref[...], tok, preferred_element_type=F32).astype(BF16)
    o_ref[...] = out.reshape(SUBLANES, S5_STEPS, D_TOK)


def _s5_glu(proj, a_re, a_im, log_dt, b_re, b_im, c_re, c_im, d_skip, w_glu):
    bsz, seq, _ = proj.shape
    assert bsz == SUBLANES, "the scan keeps one batch element per sublane"
    rows = SUBLANES * S5_STEPS
    ab_re, ab_im, b_bd, c_bd = _s5_discretise(a_re, a_im, log_dt, b_re, b_im, c_re, c_im)
    src = (jnp.arange(rows) % SUBLANES) * S5_STEPS + jnp.arange(rows) // SUBLANES
    perm = (src[:, None] == jnp.arange(rows)[None, :]).astype(BF16)
    return pl.pallas_call(
        _s5_glu_kernel,
        grid=(seq // S5_STEPS,),
        in_specs=[
            pl.BlockSpec((bsz, S5_STEPS, D_TOK), lambda i: (0, i, 0)),
            _const_spec((rows, rows)),
            _const_spec((rows, rows)),
            _const_spec((S5_SLABS, MXU_DIM, 2 * S5_SLAB_STATE)),
            _const_spec((S5_SLABS, 2 * S5_SLAB_STATE, MXU_DIM)),
            _const_spec((1, S5_STATE_ALL)),
            _const_spec((1, S5_STATE_ALL)),
            _const_spec((1, D_TOK)),
            _const_spec((D_TOK, D_TOK)),
        ],
        out_specs=pl.BlockSpec((bsz, S5_STEPS, D_TOK), lambda i: (0, i, 0)),
        out_shape=jax.ShapeDtypeStruct((bsz, seq, D_TOK), BF16),
        scratch_shapes=[
            pltpu.VMEM((SUBLANES, S5_STATE_ALL), F32),
            pltpu.VMEM((SUBLANES, S5_STATE_ALL), F32),
            pltpu.VMEM((rows, S5_STATE_ALL), F32),
            pltpu.VMEM((rows, S5_STATE_ALL), F32),
        ],
        compiler_params=pltpu.CompilerParams(dimension_semantics=("arbitrary",),
                                             vmem_limit_bytes=VMEM_LIMIT),
        name="s5_glu",
    )(proj, perm, perm.T, b_bd, c_bd, ab_re, ab_im, d_skip.reshape(1, D_TOK).astype(F32),
      w_glu.astype(BF16))


def _split3(x):
    hi = x.astype(BF16)
    r = x - hi.astype(F32)
    mid = r.astype(BF16)
    lo = (r - mid.astype(F32)).astype(BF16)
    return hi, mid, lo


def _kv_fgate_kernel(x_ref, g_ref, wkv_ref, wf_ref, bf_ref, tri_ref, kv_ref, qb_ref, kb_ref, carry_ref):
    @pl.when(pl.program_id(1) == 0)
    def _():
        carry_ref[...] = jnp.zeros_like(carry_ref)

    hs = _rmsnorm(x_ref[0], g_ref[...]).astype(BF16)
    kv_ref[0] = jnp.dot(hs, wkv_ref[...], preferred_element_type=F32).astype(BF16)
    z = jnp.dot(hs, wf_ref[...], preferred_element_type=F32) + bf_ref[...]
    logf = -(jnp.maximum(-z, 0.0) + jnp.log1p(jnp.exp(-jnp.abs(z))))
    tri = tri_ref[...]
    csum = sum(jnp.dot(tri, part, preferred_element_type=F32) for part in _split3(logf))
    fcum = csum + carry_ref[...]
    carry_ref[...] = fcum[KV_TILE - 1:KV_TILE, :]
    hi, mid, lo = (part.astype(F32) for part in _split3(fcum * LOG2E))
    lane = lax.broadcasted_iota(jnp.int32, fcum.shape, 1)
    j = lane & (GATE_COLS - 1)
    valid = lane < N_FOX_HEADS * GATE_COLS
    one = jnp.ones_like(hi)
    zero = jnp.zeros_like(hi)
    qb = jnp.where(j == 0, hi, jnp.where(j == 1, mid, jnp.where(j == 2, lo, jnp.where(j < 6, one, zero))))
    kb = jnp.where(j < 3, one, jnp.where(j == 3, -hi, jnp.where(j == 4, -mid, jnp.where(j == 5, -lo, zero))))
    qb_ref[0] = jnp.where(valid, qb, zero).astype(BF16)
    kb_ref[0] = jnp.where(valid, kb, zero).astype(BF16)


def _kv_fgate(h, g_kv, w_kv, w_fgate, b_fgate):
    bsz, seq, _ = h.shape
    pad = LANES - N_FOX_HEADS * GATE_COLS
    wf = jnp.pad(jnp.repeat(w_fgate, GATE_COLS, axis=1), ((0, 0), (0, pad))).astype(BF16)
    bfr = jnp.pad(jnp.repeat(b_fgate, GATE_COLS), (0, pad)).reshape(1, LANES).astype(F32)
    tri = (jnp.arange(KV_TILE)[:, None] >= jnp.arange(KV_TILE)[None, :]).astype(BF16)
    return pl.pallas_call(
        _kv_fgate_kernel,
        grid=(bsz, seq // KV_TILE),
        in_specs=[
            pl.BlockSpec((1, KV_TILE, D_MODEL), lambda b, t: (b, t, 0)),
            _const_spec((1, D_MODEL)),
            _const_spec((D_MODEL, 2 * D_TOK)),
            _const_spec((D_MODEL, LANES)),
            _const_spec((1, LANES)),
            _const_spec((KV_TILE, KV_TILE)),
        ],
        out_specs=[
            pl.BlockSpec((1, KV_TILE, 2 * D_TOK), lambda b, t: (b, t, 0)),
            pl.BlockSpec((1, KV_TILE, LANES), lambda b, t: (b, t, 0)),
            pl.BlockSpec((1, KV_TILE, LANES), lambda b, t: (b, t, 0)),
        ],
        out_shape=[
            jax.ShapeDtypeStruct((bsz, seq, 2 * D_TOK), BF16),
            jax.ShapeDtypeStruct((bsz, seq, LANES), BF16),
            jax.ShapeDtypeStruct((bsz, seq, LANES), BF16),
        ],
        scratch_shapes=[pltpu.VMEM((1, LANES), F32)],
        compiler_params=pltpu.CompilerParams(dimension_semantics=("arbitrary", "arbitrary")),
        name="kv_fgate",
    )(h, g_kv.reshape(1, D_MODEL), w_kv.astype(BF16), wf, bfr, tri)


def _fox_kernel(itab_ref, jtab_ref, q_ref, qb_ref, k_ref, v_ref, kb_ref, o_ref,
                qq_ref, m_ref, acc_ref, s_ref, p_ref, a_ref, *, n_steps, n_diag):
    hp = pl.program_id(1)
    seq = q_ref.shape[1]
    n_q = seq // FOX_TQ
    rows = 2 * FOX_TQ
    n_lane_blocks = FOX_TK // LANES
    ones_cols = jnp.ones((FOX_TK, LANES), BF16)

    def scores(i, j):
        c0 = pl.multiple_of(j * FOX_TK, FOX_TK)
        kk = jnp.concatenate([k_ref[0, pl.ds(c0, FOX_TK), :], kb_ref[0, pl.ds(c0, FOX_TK), :]], axis=1)
        return lax.dot_general(qq_ref[i], kk, (((1,), (1,)), ((), ())), preferred_element_type=F32)

    def lane_blocks(s):
        return [s[:, n * LANES:(n + 1) * LANES] for n in range(n_lane_blocks)]

    def row_max(blocks):
        mx = functools.reduce(jnp.maximum, blocks)
        return jnp.broadcast_to(jnp.max(mx, axis=1, keepdims=True), (rows, LANES))

    def probs(blocks, m):
        return jnp.concatenate([jnp.exp2(blk - m).astype(BF16) for blk in blocks], axis=1)

    def pv_dot(p, j):
        c0 = pl.multiple_of(j * FOX_TK, FOX_TK)
        vv = jnp.concatenate([v_ref[0, pl.ds(c0, FOX_TK), :], ones_cols], axis=1)
        return jnp.dot(p, vv, preferred_element_type=F32)

    lane = lax.broadcasted_iota(jnp.int32, (FOX_TQ, LANES), 1)
    first = lane < HEAD_DIM
    gate_head = lane // GATE_COLS

    def build_q(i, carry):
        r0 = pl.multiple_of(i * FOX_TQ, FOX_TQ)
        q2 = q_ref[0, pl.ds(r0, FOX_TQ), :].astype(F32)
        qb = qb_ref[0, pl.ds(r0, FOX_TQ), :].astype(F32)
        zero = jnp.zeros_like(q2)
        qa = jnp.concatenate([jnp.where(first, q2, zero), jnp.where(gate_head == 2 * hp, qb, zero)], axis=1)
        qo = jnp.concatenate([jnp.where(first, zero, q2), jnp.where(gate_head == 2 * hp + 1, qb, zero)], axis=1)
        qq_ref[i] = jnp.concatenate([qa, qo], axis=0).astype(BF16)
        m_ref[i] = jnp.full((rows, LANES), -jnp.inf, F32)
        acc_ref[i] = jnp.zeros((rows, 2 * LANES), F32)
        return carry

    lax.fori_loop(0, n_q, build_q, 0)

    row = lax.broadcasted_iota(jnp.int32, (rows, FOX_TK), 0) & (FOX_TQ - 1)
    col = lax.broadcasted_iota(jnp.int32, (rows, FOX_TK), 1)

    def stage_scores(t):
        s_ref[t % 2] = scores(itab_ref[t], jtab_ref[t])

    def stage_softmax(t, masked):
        slot = t % 2
        i = itab_ref[t]
        s = s_ref[slot]
        if masked:
            s = jnp.where(i * FOX_TQ + row >= jtab_ref[t] * FOX_TK + col, s, -jnp.inf)
        blocks = lane_blocks(s)
        m_old = m_ref[i]
        m_new = jnp.maximum(m_old, row_max(blocks))
        m_ref[i] = m_new
        a_ref[slot] = jnp.exp2(m_old - m_new)
        p_ref[slot] = probs(blocks, m_new)

    def stage_pv(t):
        slot = t % 2
        i = itab_ref[t]
        alpha = a_ref[slot]
        acc_ref[i] = jnp.concatenate([alpha, alpha], axis=1) * acc_ref[i] + pv_dot(p_ref[slot], jtab_ref[t])

    def tick(t, parity, masked):
        assert parity in (0, 1)
        s_slot, p_slot = parity, 1 - parity
        s_ref[s_slot] = scores(itab_ref[t + 2], jtab_ref[t + 2])
        i1 = itab_ref[t + 1]
        s = s_ref[p_slot]
        if masked:
            s = jnp.where(i1 * FOX_TQ + row >= jtab_ref[t + 1] * FOX_TK + col, s, -jnp.inf)
        blocks = lane_blocks(s)
        i0 = itab_ref[t]
        alpha = a_ref[s_slot]
        acc_ref[i0] = (jnp.concatenate([alpha, alpha], axis=1) * acc_ref[i0]
                       + pv_dot(p_ref[s_slot], jtab_ref[t]))
        m_old = m_ref[i1]
        m_new = jnp.maximum(m_old, row_max(blocks))
        m_ref[i1] = m_new
        a_ref[p_slot] = jnp.exp2(m_old - m_new)
        p_ref[p_slot] = probs(blocks, m_new)

    def ticks(first_t, count, unroll, masked):
        assert count % unroll == 0 and unroll % 2 == 0 and first_t % 2 == 0

        def body(u, carry):
            for k in range(unroll):
                tick(first_t + unroll * u + k, k % 2, masked)
            return carry

        lax.fori_loop(0, count // unroll, body, 0)

    assert n_diag % 2 == 0 and n_diag >= 4 and (n_steps - n_diag - 4) % 4 == 0
    stage_scores(0)
    stage_scores(1)
    stage_softmax(0, True)
    ticks(0, n_diag - 2, 2, True)
    tick(n_diag - 2, 0, True)
    tick(n_diag - 1, 1, False)
    ticks(n_diag, n_steps - n_diag - 4, 4, False)
    tick(n_steps - 4, 0, False)
    tick(n_steps - 3, 1, False)
    stage_softmax(n_steps - 1, False)
    stage_pv(n_steps - 2)
    stage_pv(n_steps - 1)

    def finish(i, carry):
        acc = acc_ref[i]
        o = acc[:, :LANES] / acc[:, LANES:]
        r0 = pl.multiple_of(i * FOX_TQ, FOX_TQ)
        o_ref[0, pl.ds(r0, FOX_TQ), :] = jnp.where(first, o[:FOX_TQ], o[FOX_TQ:]).astype(BF16)
        return carry

    lax.fori_loop(0, n_q, finish, 0)


def _fox(proj, kv, qb, kb):
    bsz, seq, _ = proj.shape
    n_pairs = D_TOK // LANES
    n_q = seq // FOX_TQ
    rows = 2 * FOX_TQ
    diag = [(i, (i * FOX_TQ) // FOX_TK) for i in range(n_q)]
    pairs = diag + [(i, j) for j in range(seq // FOX_TK) for i in range(n_q) if (i * FOX_TQ) // FOX_TK > j]
    itab = jnp.asarray([p[0] for p in pairs], jnp.int32)
    jtab = jnp.asarray([p[1] for p in pairs], jnp.int32)
    grid_spec = pltpu.PrefetchScalarGridSpec(
        num_scalar_prefetch=2,
        grid=(bsz, n_pairs),
        in_specs=[
            pl.BlockSpec((1, seq, LANES), lambda b, p, *_: (b, 0, p)),
            pl.BlockSpec((1, seq, LANES), lambda b, p, *_: (b, 0, 0)),
            pl.BlockSpec((1, seq, LANES), lambda b, p, *_: (b, 0, p)),
            pl.BlockSpec((1, seq, LANES), lambda b, p, *_: (b, 0, n_pairs + p)),
            pl.BlockSpec((1, seq, LANES), lambda b, p, *_: (b, 0, 0)),
        ],
        out_specs=pl.BlockSpec((1, seq, LANES), lambda b, p, *_: (b, 0, p)),
        scratch_shapes=[
            pltpu.VMEM((n_q, rows, 2 * LANES), BF16),
            pltpu.VMEM((n_q, rows, LANES), F32),
            pltpu.VMEM((n_q, rows, 2 * LANES), F32),
            pltpu.VMEM((2, rows, FOX_TK), F32),
            pltpu.VMEM((2, rows, FOX_TK), BF16),
            pltpu.VMEM((2, rows, LANES), F32),
        ],
    )
    return pl.pallas_call(
        functools.partial(_fox_kernel, n_steps=len(pairs), n_diag=len(diag)),
        grid_spec=grid_spec,
        out_shape=jax.ShapeDtypeStruct((bsz, seq, D_TOK), BF16),
        compiler_params=pltpu.CompilerParams(dimension_semantics=("arbitrary", "arbitrary"),
                                             vmem_limit_bytes=VMEM_LIMIT),
        name="fox",
    )(itab, jtab, proj, qb, kv, kv, kb)


def _mix_ffn_kernel(h_ref, tok_ref, qm_ref, mkv_ref, wout_ref, gffn_ref, wup_ref, cw_ref, wdn_ref,
                    gfin_ref, o_ref, carry_ref, acc_ref, xn_ref, up_ref, act_ref, *, final_norm):
    tm = h_ref.shape[1]

    qm = qm_ref[0].astype(F32) * ((D_MEM // N_MEM_HEADS) ** -0.5)
    km = mkv_ref[0, :, :D_MEM]
    vm = mkv_ref[0, :, D_MEM:]
    lane = lax.broadcasted_iota(jnp.int32, (tm, D_MEM), 1)
    mem = jnp.zeros((tm, D_MEM), F32)
    for hh in range(N_MEM_HEADS):
        sel = (lane >= hh * HEAD_DIM) & (lane < (hh + 1) * HEAD_DIM)
        qh = jnp.where(sel, qm, jnp.zeros_like(qm)).astype(BF16)
        s = lax.dot_general(qh, km, (((1,), (1,)), ((), ())), preferred_element_type=F32)
        p = jnp.exp(s - jnp.max(s, axis=1, keepdims=True))
        o = jnp.dot(p.astype(BF16), vm, preferred_element_type=F32)
        mem = jnp.where(sel, o / jnp.sum(p, axis=1, keepdims=True), mem)

    cat = jnp.concatenate([tok_ref[0], mem.astype(BF16)], axis=1)
    hmid = h_ref[0] + jnp.dot(cat, wout_ref[...], preferred_element_type=F32)
    acc_ref[...] = hmid
    xn = _rmsnorm(hmid, gffn_ref[...]).astype(BF16)

    @pl.when(pl.program_id(1) == 0)
    def _():
        carry_ref[...] = jnp.zeros_like(carry_ref)

    xn_ref[...] = xn
    row8 = lax.broadcasted_iota(jnp.int32, (SUBLANES, FF_CHUNK), 0)
    up_ref[0] = jnp.dot(xn_ref[...], wup_ref[0], preferred_element_type=F32)
    for c in range(N_FF_CHUNKS):
        if c + 1 < N_FF_CHUNKS:
            up_ref[(c + 1) % 2] = jnp.dot(xn_ref[...], wup_ref[c + 1], preferred_element_type=F32)
        a = up_ref[c % 2, :, :FF_CHUNK]
        g = up_ref[c % 2, :, FF_CHUNK:]
        prev = carry_ref[c]
        carry_ref[c] = g[tm - SUBLANES:, :]
        g1 = pltpu.roll(g, 1, 0)
        g2 = pltpu.roll(g, 2, 0)
        g1 = jnp.concatenate([jnp.where(row8 < 1, pltpu.roll(prev, 1, 0), g1[:SUBLANES]), g1[SUBLANES:]], axis=0)
        g2 = jnp.concatenate([jnp.where(row8 < 2, pltpu.roll(prev, 2, 0), g2[:SUBLANES]), g2[SUBLANES:]], axis=0)
        cw = cw_ref[c]
        gc = g2 * cw[0:1] + g1 * cw[1:2] + g * cw[2:3] + cw[3:4]
        act_ref[:, c * FF_CHUNK:(c + 1) * FF_CHUNK] = (gc * jax.nn.sigmoid(gc) * a).astype(BF16)

    out = acc_ref[...] + jnp.dot(act_ref[...], wdn_ref[...], preferred_element_type=F32)
    if final_norm:
        out = _rmsnorm(out, gfin_ref[...])
    o_ref[0] = out


def _mix_ffn(h, tok, proj, memkv, w_out, g_ffn, w_up, conv_w, conv_b, w_down, g_final, final_norm):
    bsz, seq, _ = h.shape
    tm = ROW_TILE
    wa = w_up[:, :D_FF].reshape(D_MODEL, N_FF_CHUNKS, FF_CHUNK)
    wg = w_up[:, D_FF:].reshape(D_MODEL, N_FF_CHUNKS, FF_CHUNK)
    wup = jnp.concatenate([wa, wg], axis=2).transpose(1, 0, 2).astype(BF16)
    wdn = w_down.astype(BF16)
    cw = jnp.concatenate([conv_w, conv_b[None, :], jnp.zeros((SUBLANES - 4, D_FF), conv_w.dtype)], axis=0)
    cw = cw.reshape(SUBLANES, N_FF_CHUNKS, FF_CHUNK).transpose(1, 0, 2).astype(F32)
    return pl.pallas_call(
        functools.partial(_mix_ffn_kernel, final_norm=final_norm),
        grid=(bsz, seq // tm),
        in_specs=[
            pl.BlockSpec((1, tm, D_MODEL), lambda b, t: (b, t, 0)),
            pl.BlockSpec((1, tm, D_TOK), lambda b, t: (b, t, 0)),
            pl.BlockSpec((1, tm, D_MEM), lambda b, t: (b, t, D_TOK // D_MEM)),
            pl.BlockSpec((1, MEM_TOKENS, 2 * D_MEM), lambda b, t: (b, 0, 0)),
            _const_spec((D_MODEL, D_MODEL)),
            _const_spec((1, D_MODEL)),
            _const_spec((N_FF_CHUNKS, D_MODEL, 2 * FF_CHUNK)),
            _const_spec((N_FF_CHUNKS, SUBLANES, FF_CHUNK)),
            _const_spec((D_FF, D_MODEL)),
            _const_spec((1, D_MODEL)),
        ],
        out_specs=pl.BlockSpec((1, tm, D_MODEL), lambda b, t: (b, t, 0)),
        out_shape=jax.ShapeDtypeStruct((bsz, seq, D_MODEL), F32),
        scratch_shapes=[
            pltpu.VMEM((N_FF_CHUNKS, SUBLANES, FF_CHUNK), F32),
            pltpu.VMEM((tm, D_MODEL), F32),
            pltpu.VMEM((tm, D_MODEL), BF16),
            pltpu.VMEM((2, tm, 2 * FF_CHUNK), F32),
            pltpu.VMEM((tm, D_FF), BF16),
        ],
        compiler_params=pltpu.CompilerParams(dimension_semantics=("arbitrary", "arbitrary"),
                                             vmem_limit_bytes=VMEM_LIMIT),
        name="mix_ffn",
    )(h, tok, proj, memkv, w_out.astype(BF16), g_ffn.reshape(1, D_MODEL), wup, cw, wdn,
      g_final.reshape(1, D_MODEL))


def kernel(x, mem, g_mix, w_in, w_out, g_mem, w_mem_kv, s5_a_re, s5_a_im, s5_log_dt, s5_b_re, s5_b_im,
           s5_c_re, s5_c_im, s5_d, w_glu, g_kv, w_kv, w_fgate, b_fgate, g_ffn, w_ffn_up, conv_w, conv_b,
           w_ffn_down, g_final):
    bsz, seq, _ = x.shape
    depth = w_in.shape[0]
    n_a = depth // 2
    memkv = _memkv(mem, g_mem, w_mem_kv)
    h = x
    kv = qb = kb = None
    for l in range(depth):
        w_l = w_in[l]
        if l >= n_a:
            q_scale = jnp.where(jnp.arange(D_MODEL) < D_TOK, (HEAD_DIM ** -0.5) * LOG2E, 1.0).astype(F32)
            w_l = w_l * q_scale[None, :]
        proj = _rms_proj(h.reshape(bsz * seq, D_MODEL), g_mix[l], w_l).reshape(bsz, seq, D_MODEL)
        if l < n_a:
            tok = _s5_glu(proj, s5_a_re[l], s5_a_im[l], s5_log_dt[l], s5_b_re[l], s5_b_im[l],
                          s5_c_re[l], s5_c_im[l], s5_d[l], w_glu[l])
        else:
            tok = _fox(proj, kv, qb, kb)
        h = _mix_ffn(h, tok, proj, memkv[l], w_out[l], g_ffn[l], w_ffn_up[l], conv_w[l], conv_b[l],
                     w_ffn_down[l], g_final, final_norm=(l == depth - 1))
        if l == n_a - 1:
            kv, qb, kb = _kv_fgate(h, g_kv, w_kv, w_fgate, b_fgate)
    return h
```

```python
import functools
import math

import jax
import jax.numpy as jnp
from jax import lax
from jax.experimental import pallas as pl
from jax.experimental.pallas import tpu as pltpu

F32 = jnp.float32
BF16 = jnp.bfloat16

D_MODEL = 1024
HEAD_DIM = 64
D_MEM = 256
N_MEM_HEADS = 4
MEM_TOKENS = 256
D_TOK = D_MODEL - D_MEM
S5_GROUP = 16
S5_GROUPS = D_TOK // S5_GROUP
S5_STATE = 64
N_FOX_HEADS = D_TOK // HEAD_DIM
D_FF = 2816
EPS = 1e-6
LOG2E = math.log2(math.e)

LANES = 128
SUBLANES = 8
MXU_DIM = 256

ROW_TILE = 512
PROJ_TILE = 1024
S5_STEPS = MXU_DIM // SUBLANES
S5_SLABS = D_TOK // MXU_DIM
S5_SLAB_STATE = (MXU_DIM // S5_GROUP) * S5_STATE
S5_STATE_ALL = S5_GROUPS * S5_STATE
S5_STAGES = 2
SCAN_LANES = 512
FF_CHUNK = 256
N_FF_CHUNKS = D_FF // FF_CHUNK
FOX_TQ = 256
FOX_TK = 512
FOX_UNROLL = 8
FOX_UNROLL_DIAG = 4
GATE_COLS = 8
KV_TILE = 512
CUMSUM_BLOCK = MXU_DIM
VMEM_LIMIT = 56 * 1024 * 1024


def _rmsnorm(x, g):
    ms = jnp.mean(x * x, axis=-1, keepdims=True)
    return x * lax.rsqrt(ms + EPS) * g


def _const_spec(shape):
    zeros = (0,) * len(shape)
    return pl.BlockSpec(shape, lambda *_: zeros, pipeline_mode=pl.Buffered(1))


def _memkv_kernel(mem_ref, g_ref, w_ref, o_ref):
    mn = _rmsnorm(mem_ref[0], g_ref[...]).astype(BF16)
    for l in range(w_ref.shape[0]):
        o_ref[l, 0] = jnp.dot(mn, w_ref[l], preferred_element_type=F32).astype(BF16)


def _memkv(mem, g_mem, w_mem_kv):
    bsz = mem.shape[0]
    depth = w_mem_kv.shape[0]
    return pl.pallas_call(
        _memkv_kernel,
        grid=(bsz,),
        in_specs=[
            pl.BlockSpec((1, MEM_TOKENS, D_MODEL), lambda b: (b, 0, 0)),
            _const_spec((1, D_MODEL)),
            _const_spec((depth, D_MODEL, 2 * D_MEM)),
        ],
        out_specs=pl.BlockSpec((depth, 1, MEM_TOKENS, 2 * D_MEM), lambda b: (0, b, 0, 0)),
        out_shape=jax.ShapeDtypeStruct((depth, bsz, MEM_TOKENS, 2 * D_MEM), BF16),
        compiler_params=pltpu.CompilerParams(dimension_semantics=("arbitrary",)),
        name="memkv",
    )(mem, g_mem.reshape(1, D_MODEL), w_mem_kv.astype(BF16))


def _rms_proj_kernel(x_ref, g_ref, w_ref, o_ref):
    xn = _rmsnorm(x_ref[...], g_ref[...]).astype(BF16)
    o_ref[...] = jnp.dot(xn, w_ref[...], preferred_element_type=F32).astype(o_ref.dtype)


def _rms_proj(h2d, g, w):
    rows = h2d.shape[0]
    n = w.shape[1]
    return pl.pallas_call(
        _rms_proj_kernel,
        grid=(rows // PROJ_TILE,),
        in_specs=[
            pl.BlockSpec((PROJ_TILE, D_MODEL), lambda i: (i, 0)),
            _const_spec((1, D_MODEL)),
            _const_spec((D_MODEL, n)),
        ],
        out_specs=pl.BlockSpec((PROJ_TILE, n), lambda i: (i, 0)),
        out_shape=jax.ShapeDtypeStruct((rows, n), BF16),
        compiler_params=pltpu.CompilerParams(dimension_semantics=("arbitrary",)),
        name="rms_proj",
    )(h2d, g.reshape(1, D_MODEL), w.astype(BF16))


def _s5_discretise(a_re, a_im, log_dt, b_re, b_im, c_re, c_im):
    dt = jnp.exp(log_dt.astype(F32))[:, None]
    lam_re = jnp.minimum(a_re.astype(F32), -1e-4)
    lam_im = a_im.astype(F32)
    mag = jnp.exp(lam_re * dt)
    ph = lam_im * dt
    ab_re, ab_im = mag * jnp.cos(ph), mag * jnp.sin(ph)
    den = lam_re * lam_re + lam_im * lam_im
    z_re = ((ab_re - 1.0) * lam_re + ab_im * lam_im) / den
    z_im = (ab_im * lam_re - (ab_re - 1.0) * lam_im) / den
    br, bi = b_re.astype(F32), b_im.astype(F32)
    bb_re = z_re[..., None] * br - z_im[..., None] * bi
    bb_im = z_re[..., None] * bi + z_im[..., None] * br
    gps = MXU_DIM // S5_GROUP
    eye = jnp.eye(gps, dtype=F32)

    def pack_b(bb):
        bb = bb.reshape(S5_SLABS, gps, S5_STATE, S5_GROUP)
        return jnp.einsum('sgpi,gh->sgihp', bb, eye).reshape(S5_SLABS, MXU_DIM, S5_SLAB_STATE)

    def pack_c(cc):
        cc = cc.reshape(S5_SLABS, gps, S5_GROUP, S5_STATE)
        return jnp.einsum('sgip,gh->sgphi', cc, eye).reshape(S5_SLABS, S5_SLAB_STATE, MXU_DIM)

    b_bd = jnp.concatenate([pack_b(bb_re), pack_b(bb_im)], axis=2).astype(BF16)
    c_bd = jnp.concatenate([pack_c(c_re.astype(F32)), -pack_c(c_im.astype(F32))], axis=1).astype(BF16)
    return ab_re.reshape(1, S5_STATE_ALL), ab_im.reshape(1, S5_STATE_ALL), b_bd, c_bd


def _s5_glu_kernel(x_ref, gmix_ref, win_ref, perm_ref, permt_ref, bbd_ref, cbd_ref, are_ref, aim_ref,
                   d_ref, wglu_ref, o_ref, qm_ref, hre_ref, him_ref, *bufs):
    rows = SUBLANES * S5_STEPS
    sre_refs, sim_refs, up_refs = bufs[0:S5_STAGES], bufs[S5_STAGES:2 * S5_STAGES], bufs[2 * S5_STAGES:]
    i = pl.program_id(0)

    @pl.when(i == 0)
    def _():
        hre_ref[...] = jnp.zeros_like(hre_ref)
        him_ref[...] = jnp.zeros_like(him_ref)
        for ref in bufs:
            ref[...] = jnp.zeros_like(ref)

    def permute_in(up_ref):
        xn = _rmsnorm(x_ref[...].reshape(rows, D_MODEL), gmix_ref[...]).astype(BF16)
        proj = jnp.dot(xn, win_ref[...], preferred_element_type=F32).astype(BF16)
        qm_ref[...] = proj[:, D_TOK:].reshape(SUBLANES, S5_STEPS, D_MEM)
        up_ref[...] = jnp.dot(perm_ref[...], proj[:, :D_TOK], preferred_element_type=F32).astype(BF16)

    def project_in(s, sre_ref, sim_ref, up_ref):
        bu = jnp.dot(up_ref[:, s * MXU_DIM:(s + 1) * MXU_DIM], bbd_ref[s], preferred_element_type=F32)
        sre_ref[:, s * S5_SLAB_STATE:(s + 1) * S5_SLAB_STATE] = bu[:, :S5_SLAB_STATE]
        sim_ref[:, s * S5_SLAB_STATE:(s + 1) * S5_SLAB_STATE] = bu[:, S5_SLAB_STATE:]

    def scan(c, sre_ref, sim_ref):
        cs = slice(c * SCAN_LANES, (c + 1) * SCAN_LANES)
        ar = jnp.broadcast_to(are_ref[:, cs], (SUBLANES, SCAN_LANES))
        ai = jnp.broadcast_to(aim_ref[:, cs], (SUBLANES, SCAN_LANES))
        hr = hre_ref[:, cs]
        hi = him_ref[:, cs]
        for t in range(S5_STEPS):
            rs = slice(t * SUBLANES, (t + 1) * SUBLANES)
            br = sre_ref[rs, cs]
            bi = sim_ref[rs, cs]
            hr, hi = ar * hr - ai * hi + br, ar * hi + ai * hr + bi
            sre_ref[rs, cs] = hr
            sim_ref[rs, cs] = hi
        hre_ref[:, cs] = hr
        him_ref[:, cs] = hi

    def project_out(s, sre_ref, sim_ref):
        ss = slice(s * S5_SLAB_STATE, (s + 1) * S5_SLAB_STATE)
        hcat = jnp.concatenate([sre_ref[:, ss], sim_ref[:, ss]], axis=1).astype(BF16)
        return jnp.dot(hcat, cbd_ref[s], preferred_element_type=F32)

    def gate_out(ys, up_ref):
        y = jnp.concatenate(ys, axis=1) + d_ref[...] * up_ref[...].astype(F32)
        g = jax.nn.gelu(y)
        z = jnp.dot(g.astype(BF16), wglu_ref[...], preferred_element_type=F32)
        tok = (g * jax.nn.sigmoid(z)).astype(BF16)
        out = jnp.dot(permt_ref[...], tok, preferred_element_type=F32).astype(BF16)
        o_ref[...] = out.reshape(SUBLANES, S5_STEPS, D_TOK)

    scans_per_slab = S5_STATE_ALL // SCAN_LANES // S5_SLABS
    for r in range(S5_STAGES):
        @pl.when(i % S5_STAGES == r)
        def _(r=r):
            a, b = r, (r - 1) % S5_STAGES
            permute_in(up_refs[a])
            ys = []
            for s in range(S5_SLABS):
                project_in(s, sre_refs[a], sim_refs[a], up_refs[a])
                for k in range(scans_per_slab):
                    scan(s * scans_per_slab + k, sre_refs[b], sim_refs[b])
                ys.append(project_out(s, sre_refs[b], sim_refs[b]))
            gate_out(ys, up_refs[b])


def _s5_glu(h, g_mix, w_in, a_re, a_im, log_dt, b_re, b_im, c_re, c_im, d_skip, w_glu):
    bsz, seq, _ = h.shape
    assert bsz == SUBLANES, "the scan keeps one batch element per sublane"
    rows = SUBLANES * S5_STEPS
    n_blocks = seq // S5_STEPS
    lag = S5_STAGES - 1
    ab_re, ab_im, b_bd, c_bd = _s5_discretise(a_re, a_im, log_dt, b_re, b_im, c_re, c_im)
    src = (jnp.arange(rows) % SUBLANES) * S5_STEPS + jnp.arange(rows) // SUBLANES
    perm = (src[:, None] == jnp.arange(rows)[None, :]).astype(BF16)
    stage_bufs = ([pltpu.VMEM((rows, S5_STATE_ALL), F32)] * (2 * S5_STAGES)
                  + [pltpu.VMEM((rows, D_TOK), BF16)] * S5_STAGES)
    return pl.pallas_call(
        _s5_glu_kernel,
        grid=(n_blocks + lag,),
        in_specs=[
            pl.BlockSpec((bsz, S5_STEPS, D_MODEL), lambda i: (0, jnp.minimum(i, n_blocks - 1), 0)),
            _const_spec((1, D_MODEL)),
            _const_spec((D_MODEL, D_MODEL)),
            _const_spec((rows, rows)),
            _const_spec((rows, rows)),
            _const_spec((S5_SLABS, MXU_DIM, 2 * S5_SLAB_STATE)),
            _const_spec((S5_SLABS, 2 * S5_SLAB_STATE, MXU_DIM)),
            _const_spec((1, S5_STATE_ALL)),
            _const_spec((1, S5_STATE_ALL)),
            _const_spec((1, D_TOK)),
            _const_spec((D_TOK, D_TOK)),
        ],
        out_specs=[
            pl.BlockSpec((bsz, S5_STEPS, D_TOK), lambda i: (0, jnp.maximum(i - lag, 0), 0)),
            pl.BlockSpec((bsz, S5_STEPS, D_MEM), lambda i: (0, jnp.minimum(i, n_blocks - 1), 0)),
        ],
        out_shape=[
            jax.ShapeDtypeStruct((bsz, seq, D_TOK), BF16),
            jax.ShapeDtypeStruct((bsz, seq, D_MEM), BF16),
        ],
        scratch_shapes=[
            pltpu.VMEM((SUBLANES, S5_STATE_ALL), F32),
            pltpu.VMEM((SUBLANES, S5_STATE_ALL), F32),
        ] + stage_bufs,
        compiler_params=pltpu.CompilerParams(dimension_semantics=("arbitrary",),
                                             vmem_limit_bytes=VMEM_LIMIT),
        name="s5_glu",
    )(h, g_mix.reshape(1, D_MODEL), w_in.astype(BF16), perm, perm.T, b_bd, c_bd, ab_re, ab_im,
      d_skip.reshape(1, D_TOK).astype(F32), w_glu.astype(BF16))


def _split3(x):
    hi = x.astype(BF16)
    r = x - hi.astype(F32)
    mid = r.astype(BF16)
    lo = (r - mid.astype(F32)).astype(BF16)
    return hi, mid, lo


def _kv_fgate_kernel(x_ref, g_ref, wkv_ref, wf_ref, bf_ref, tri_ref, kv_ref, qb_ref, kb_ref, carry_ref):
    @pl.when(pl.program_id(1) == 0)
    def _():
        carry_ref[...] = jnp.zeros_like(carry_ref)

    hs = _rmsnorm(x_ref[0], g_ref[...]).astype(BF16)
    kv_ref[0] = jnp.dot(hs, wkv_ref[...], preferred_element_type=F32).astype(BF16)
    z = jnp.dot(hs, wf_ref[...], preferred_element_type=F32) + bf_ref[...]
    logf = -(jnp.maximum(-z, 0.0) + jnp.log1p(jnp.exp(-jnp.abs(z))))
    tri = tri_ref[...]
    parts = _split3(logf)
    running = carry_ref[...]
    blocks = []
    for r in range(KV_TILE // CUMSUM_BLOCK):
        rs = slice(r * CUMSUM_BLOCK, (r + 1) * CUMSUM_BLOCK)
        csum = sum(jnp.dot(tri, part[rs], preferred_element_type=F32) for part in parts)
        blocks.append(csum + running)
        running = blocks[-1][CUMSUM_BLOCK - 1:CUMSUM_BLOCK, :]
    carry_ref[...] = running
    fcum = jnp.concatenate(blocks, axis=0)
    hi, mid, lo = (part.astype(F32) for part in _split3(fcum * LOG2E))
    lane = lax.broadcasted_iota(jnp.int32, fcum.shape, 1)
    j = lane & (GATE_COLS - 1)
    valid = lane < N_FOX_HEADS * GATE_COLS
    one = jnp.ones_like(hi)
    zero = jnp.zeros_like(hi)
    qb = jnp.where(j == 0, hi, jnp.where(j == 1, mid, jnp.where(j == 2, lo, jnp.where(j < 6, one, zero))))
    kb = jnp.where(j < 3, one, jnp.where(j == 3, -hi, jnp.where(j == 4, -mid, jnp.where(j == 5, -lo, zero))))
    qb_ref[0] = jnp.where(valid, qb, zero).astype(BF16)
    kb_ref[0] = jnp.where(valid, kb, zero).astype(BF16)


def _kv_fgate(h, g_kv, w_kv, w_fgate, b_fgate):
    bsz, seq, _ = h.shape
    pad = LANES - N_FOX_HEADS * GATE_COLS
    wf = jnp.pad(jnp.repeat(w_fgate, GATE_COLS, axis=1), ((0, 0), (0, pad))).astype(BF16)
    bfr = jnp.pad(jnp.repeat(b_fgate, GATE_COLS), (0, pad)).reshape(1, LANES).astype(F32)
    tri = (jnp.arange(CUMSUM_BLOCK)[:, None] >= jnp.arange(CUMSUM_BLOCK)[None, :]).astype(BF16)
    return pl.pallas_call(
        _kv_fgate_kernel,
        grid=(bsz, seq // KV_TILE),
        in_specs=[
            pl.BlockSpec((1, KV_TILE, D_MODEL), lambda b, t: (b, t, 0)),
            _const_spec((1, D_MODEL)),
            _const_spec((D_MODEL, 2 * D_TOK)),
            _const_spec((D_MODEL, LANES)),
            _const_spec((1, LANES)),
            _const_spec((CUMSUM_BLOCK, CUMSUM_BLOCK)),
        ],
        out_specs=[
            pl.BlockSpec((1, KV_TILE, 2 * D_TOK), lambda b, t: (b, t, 0)),
            pl.BlockSpec((1, KV_TILE, LANES), lambda b, t: (b, t, 0)),
            pl.BlockSpec((1, KV_TILE, LANES), lambda b, t: (b, t, 0)),
        ],
        out_shape=[
            jax.ShapeDtypeStruct((bsz, seq, 2 * D_TOK), BF16),
            jax.ShapeDtypeStruct((bsz, seq, LANES), BF16),
            jax.ShapeDtypeStruct((bsz, seq, LANES), BF16),
        ],
        scratch_shapes=[pltpu.VMEM((1, LANES), F32)],
        compiler_params=pltpu.CompilerParams(dimension_semantics=("arbitrary", "arbitrary")),
        name="kv_fgate",
    )(h, g_kv.reshape(1, D_MODEL), w_kv.astype(BF16), wf, bfr, tri)


def _fox_kernel(itab_ref, jtab_ref, q_ref, qb_ref, k_ref, v_ref, kb_ref, o_ref,
                qq_ref, m_ref, acc_ref, s_ref, p_ref, a_ref, *, n_steps, n_diag):
    hp = pl.program_id(1)
    seq = q_ref.shape[1]
    n_q = seq // FOX_TQ
    rows = 2 * FOX_TQ
    n_lane_blocks = FOX_TK // LANES
    ones_cols = jnp.ones((FOX_TK, LANES), BF16)

    def scores(i, j):
        c0 = pl.multiple_of(j * FOX_TK, FOX_TK)
        kk = jnp.concatenate([k_ref[0, pl.ds(c0, FOX_TK), :], kb_ref[0, pl.ds(c0, FOX_TK), :]], axis=1)
        return lax.dot_general(qq_ref[i], kk, (((1,), (1,)), ((), ())), preferred_element_type=F32)

    def lane_blocks(s):
        return [s[:, n * LANES:(n + 1) * LANES] for n in range(n_lane_blocks)]

    def row_max(blocks):
        mx = functools.reduce(jnp.maximum, blocks)
        return jnp.broadcast_to(jnp.max(mx, axis=1, keepdims=True), (rows, LANES))

    def probs(blocks, m):
        return jnp.concatenate([jnp.exp2(blk - m).astype(BF16) for blk in blocks], axis=1)

    def pv_dot(p, j):
        c0 = pl.multiple_of(j * FOX_TK, FOX_TK)
        vv = jnp.concatenate([v_ref[0, pl.ds(c0, FOX_TK), :], ones_cols], axis=1)
        return jnp.dot(p, vv, preferred_element_type=F32)

    lane = lax.broadcasted_iota(jnp.int32, (FOX_TQ, LANES), 1)
    first = lane < HEAD_DIM
    gate_head = lane // GATE_COLS

    def build_q(i, carry):
        r0 = pl.multiple_of(i * FOX_TQ, FOX_TQ)
        q2 = q_ref[0, pl.ds(r0, FOX_TQ), :].astype(F32)
        qb = qb_ref[0, pl.ds(r0, FOX_TQ), :].astype(F32)
        zero = jnp.zeros_like(q2)
        qa = jnp.concatenate([jnp.where(first, q2, zero), jnp.where(gate_head == 2 * hp, qb, zero)], axis=1)
        qo = jnp.concatenate([jnp.where(first, zero, q2), jnp.where(gate_head == 2 * hp + 1, qb, zero)], axis=1)
        qq_ref[i] = jnp.concatenate([qa, qo], axis=0).astype(BF16)
        m_ref[i] = jnp.full((rows, LANES), -jnp.inf, F32)
        acc_ref[i] = jnp.zeros((rows, 2 * LANES), F32)
        return carry

    lax.fori_loop(0, n_q, build_q, 0)

    row = lax.broadcasted_iota(jnp.int32, (rows, FOX_TK), 0) & (FOX_TQ - 1)
    col = lax.broadcasted_iota(jnp.int32, (rows, FOX_TK), 1)

    def stage_scores(t):
        s_ref[t % 2] = scores(itab_ref[t], jtab_ref[t])

    def stage_softmax(t, masked):
        slot = t % 2
        i = itab_ref[t]
        s = s_ref[slot]
        if masked:
            s = jnp.where(i * FOX_TQ + row >= jtab_ref[t] * FOX_TK + col, s, -jnp.inf)
        blocks = lane_blocks(s)
        m_old = m_ref[i]
        m_new = jnp.maximum(m_old, row_max(blocks))
        m_ref[i] = m_new
        a_ref[slot] = jnp.exp2(m_old - m_new)
        p_ref[slot] = probs(blocks, m_new)

    def stage_pv(t):
        slot = t % 2
        i = itab_ref[t]
        alpha = a_ref[slot]
        acc_ref[i] = jnp.concatenate([alpha, alpha], axis=1) * acc_ref[i] + pv_dot(p_ref[slot], jtab_ref[t])

    def tick(t, parity, masked):
        assert parity in (0, 1)
        s_slot, p_slot = parity, 1 - parity
        s_ref[s_slot] = scores(itab_ref[t + 2], jtab_ref[t + 2])
        i1 = itab_ref[t + 1]
        s = s_ref[p_slot]
        if masked:
            s = jnp.where(i1 * FOX_TQ + row >= jtab_ref[t + 1] * FOX_TK + col, s, -jnp.inf)
        blocks = lane_blocks(s)
        i0 = itab_ref[t]
        alpha = a_ref[s_slot]
        acc_ref[i0] = (jnp.concatenate([alpha, alpha], axis=1) * acc_ref[i0]
                       + pv_dot(p_ref[s_slot], jtab_ref[t]))
        m_old = m_ref[i1]
        m_new = jnp.maximum(m_old, row_max(blocks))
        m_ref[i1] = m_new
        a_ref[p_slot] = jnp.exp2(m_old - m_new)
        p_ref[p_slot] = probs(blocks, m_new)

    def ticks(first_t, count, unroll, masked):
        assert count % unroll == 0 and unroll % 2 == 0 and first_t % 2 == 0

        def body(u, carry):
            for k in range(unroll):
                tick(first_t + unroll * u + k, k % 2, masked)
            return carry

        lax.fori_loop(0, count // unroll, body, 0)

    n_masked = n_diag - 2
    n_plain = n_steps - n_diag - 4
    assert n_diag % 2 == 0 and n_masked >= FOX_UNROLL_DIAG and n_plain >= FOX_UNROLL
    assert (n_masked % FOX_UNROLL_DIAG) % 2 == 0 and (n_plain % FOX_UNROLL) % 2 == 0
    stage_scores(0)
    stage_scores(1)
    stage_softmax(0, True)
    main = n_masked - n_masked % FOX_UNROLL_DIAG
    ticks(0, main, FOX_UNROLL_DIAG, True)
    if n_masked > main:
        ticks(main, n_masked - main, n_masked - main, True)
    tick(n_diag - 2, 0, True)
    tick(n_diag - 1, 1, False)
    main = n_plain - n_plain % FOX_UNROLL
    ticks(n_diag, main, FOX_UNROLL, False)
    if n_plain > main:
        ticks(n_diag + main, n_plain - main, n_plain - main, False)
    tick(n_steps - 4, 0, False)
    tick(n_steps - 3, 1, False)
    stage_softmax(n_steps - 1, False)
    stage_pv(n_steps - 2)
    stage_pv(n_steps - 1)

    def finish(i, carry):
        acc = acc_ref[i]
        o = acc[:, :LANES] / acc[:, LANES:]
        r0 = pl.multiple_of(i * FOX_TQ, FOX_TQ)
        o_ref[0, pl.ds(r0, FOX_TQ), :] = jnp.where(first, o[:FOX_TQ], o[FOX_TQ:]).astype(BF16)
        return carry

    lax.fori_loop(0, n_q, finish, 0)


def _fox(proj, kv, qb, kb):
    bsz, seq, _ = proj.shape
    n_pairs = D_TOK // LANES
    n_q = seq // FOX_TQ
    rows = 2 * FOX_TQ
    diag = [(i, (i * FOX_TQ) // FOX_TK) for i in range(n_q)]
    pairs = diag + [(i, j) for j in range(seq // FOX_TK) for i in range(n_q) if (i * FOX_TQ) // FOX_TK > j]
    itab = jnp.asarray([p[0] for p in pairs], jnp.int32)
    jtab = jnp.asarray([p[1] for p in pairs], jnp.int32)
    grid_spec = pltpu.PrefetchScalarGridSpec(
        num_scalar_prefetch=2,
        grid=(bsz, n_pairs),
        in_specs=[
            pl.BlockSpec((1, seq, LANES), lambda b, p, *_: (b, 0, p)),
            pl.BlockSpec((1, seq, LANES), lambda b, p, *_: (b, 0, 0)),
            pl.BlockSpec((1, seq, LANES), lambda b, p, *_: (b, 0, p)),
            pl.BlockSpec((1, seq, LANES), lambda b, p, *_: (b, 0, n_pairs + p)),
            pl.BlockSpec((1, seq, LANES), lambda b, p, *_: (b, 0, 0)),
        ],
        out_specs=pl.BlockSpec((1, seq, LANES), lambda b, p, *_: (b, 0, p)),
        scratch_shapes=[
            pltpu.VMEM((n_q, rows, 2 * LANES), BF16),
            pltpu.VMEM((n_q, rows, LANES), F32),
            pltpu.VMEM((n_q, rows, 2 * LANES), F32),
            pltpu.VMEM((2, rows, FOX_TK), F32),
            pltpu.VMEM((2, rows, FOX_TK), BF16),
            pltpu.VMEM((2, rows, LANES), F32),
        ],
    )
    return pl.pallas_call(
        functools.partial(_fox_kernel, n_steps=len(pairs), n_diag=len(diag)),
        grid_spec=grid_spec,
        out_shape=jax.ShapeDtypeStruct((bsz, seq, D_TOK), BF16),
        compiler_params=pltpu.CompilerParams(dimension_semantics=("arbitrary", "arbitrary"),
                                             vmem_limit_bytes=VMEM_LIMIT),
        name="fox",
    )(itab, jtab, proj, qb, kv, kv, kb)


def _mix_ffn_kernel(h_ref, tok_ref, qm_ref, mkv_ref, wout_ref, gffn_ref, wup_ref, cw_ref, wdn_ref,
                    gfin_ref, o_ref, carry_ref, acc_ref, xn_ref, up_ref, act_ref, *, final_norm):
    tm = h_ref.shape[1]

    qm = qm_ref[0].astype(F32) * ((D_MEM // N_MEM_HEADS) ** -0.5)
    km = mkv_ref[0, :, :D_MEM]
    vm = mkv_ref[0, :, D_MEM:]
    lane = lax.broadcasted_iota(jnp.int32, (tm, D_MEM), 1)
    mem = jnp.zeros((tm, D_MEM), F32)
    for hh in range(N_MEM_HEADS):
        sel = (lane >= hh * HEAD_DIM) & (lane < (hh + 1) * HEAD_DIM)
        qh = jnp.where(sel, qm, jnp.zeros_like(qm)).astype(BF16)
        s = lax.dot_general(qh, km, (((1,), (1,)), ((), ())), preferred_element_type=F32)
        p = jnp.exp(s - jnp.max(s, axis=1, keepdims=True))
        o = jnp.dot(p.astype(BF16), vm, preferred_element_type=F32)
        mem = jnp.where(sel, o / jnp.sum(p, axis=1, keepdims=True), mem)

    cat = jnp.concatenate([tok_ref[0], mem.astype(BF16)], axis=1)
    hmid = h_ref[0] + jnp.dot(cat, wout_ref[...], preferred_element_type=F32)
    acc_ref[...] = hmid
    xn = _rmsnorm(hmid, gffn_ref[...]).astype(BF16)

    @pl.when(pl.program_id(1) == 0)
    def _():
        carry_ref[...] = jnp.zeros_like(carry_ref)

    xn_ref[...] = xn
    row8 = lax.broadcasted_iota(jnp.int32, (SUBLANES, FF_CHUNK), 0)
    up_ref[0] = jnp.dot(xn_ref[...], wup_ref[0], preferred_element_type=F32)
    for c in range(N_FF_CHUNKS):
        if c + 1 < N_FF_CHUNKS:
            up_ref[(c + 1) % 2] = jnp.dot(xn_ref[...], wup_ref[c + 1], preferred_element_type=F32)
        a = up_ref[c % 2, :, :FF_CHUNK]
        g = up_ref[c % 2, :, FF_CHUNK:]
        prev = carry_ref[c]
        carry_ref[c] = g[tm - SUBLANES:, :]
        g1 = pltpu.roll(g, 1, 0)
        g2 = pltpu.roll(g, 2, 0)
        g1 = jnp.concatenate([jnp.where(row8 < 1, pltpu.roll(prev, 1, 0), g1[:SUBLANES]), g1[SUBLANES:]], axis=0)
        g2 = jnp.concatenate([jnp.where(row8 < 2, pltpu.roll(prev, 2, 0), g2[:SUBLANES]), g2[SUBLANES:]], axis=0)
        cw = cw_ref[c]
        gc = g2 * cw[0:1] + g1 * cw[1:2] + g * cw[2:3] + cw[3:4]
        act_ref[:, c * FF_CHUNK:(c + 1) * FF_CHUNK] = (gc * jax.nn.sigmoid(gc) * a).astype(BF16)

    out = acc_ref[...] + jnp.dot(act_ref[...], wdn_ref[...], preferred_element_type=F32)
    if final_norm:
        out = _rmsnorm(out, gfin_ref[...])
    o_ref[0] = out


def _mix_ffn(h, tok, qmem, memkv, w_out, g_ffn, w_up, conv_w, conv_b, w_down, g_final, final_norm):
    bsz, seq, _ = h.shape
    qmem_block = qmem.shape[-1] // D_MEM - 1
    tm = ROW_TILE
    wa = w_up[:, :D_FF].reshape(D_MODEL, N_FF_CHUNKS, FF_CHUNK)
    wg = w_up[:, D_FF:].reshape(D_MODEL, N_FF_CHUNKS, FF_CHUNK)
    wup = jnp.concatenate([wa, wg], axis=2).transpose(1, 0, 2).astype(BF16)
    wdn = w_down.astype(BF16)
    cw = jnp.concatenate([conv_w, conv_b[None, :], jnp.zeros((SUBLANES - 4, D_FF), conv_w.dtype)], axis=0)
    cw = cw.reshape(SUBLANES, N_FF_CHUNKS, FF_CHUNK).transpose(1, 0, 2).astype(F32)
    return pl.pallas_call(
        functools.partial(_mix_ffn_kernel, final_norm=final_norm),
        grid=(bsz, seq // tm),
        in_specs=[
            pl.BlockSpec((1, tm, D_MODEL), lambda b, t: (b, t, 0)),
            pl.BlockSpec((1, tm, D_TOK), lambda b, t: (b, t, 0)),
            pl.BlockSpec((1, tm, D_MEM), lambda b, t: (b, t, qmem_block)),
            pl.BlockSpec((1, MEM_TOKENS, 2 * D_MEM), lambda b, t: (b, 0, 0)),
            _const_spec((D_MODEL, D_MODEL)),
            _const_spec((1, D_MODEL)),
            _const_spec((N_FF_CHUNKS, D_MODEL, 2 * FF_CHUNK)),
            _const_spec((N_FF_CHUNKS, SUBLANES, FF_CHUNK)),
            _const_spec((D_FF, D_MODEL)),
            _const_spec((1, D_MODEL)),
        ],
        out_specs=pl.BlockSpec((1, tm, D_MODEL), lambda b, t: (b, t, 0)),
        out_shape=jax.ShapeDtypeStruct((bsz, seq, D_MODEL), F32),
        scratch_shapes=[
            pltpu.VMEM((N_FF_CHUNKS, SUBLANES, FF_CHUNK), F32),
            pltpu.VMEM((tm, D_MODEL), F32),
            pltpu.VMEM((tm, D_MODEL), BF16),
            pltpu.VMEM((2, tm, 2 * FF_CHUNK), F32),
            pltpu.VMEM((tm, D_FF), BF16),
        ],
        compiler_params=pltpu.CompilerParams(dimension_semantics=("arbitrary", "arbitrary"),
                                             vmem_limit_bytes=VMEM_LIMIT),
        name="mix_ffn",
    )(h, tok, qmem, memkv, w_out.astype(BF16), g_ffn.reshape(1, D_MODEL), wup, cw, wdn,
      g_final.reshape(1, D_MODEL))


def kernel(x, mem, g_mix, w_in, w_out, g_mem, w_mem_kv, s5_a_re, s5_a_im, s5_log_dt, s5_b_re, s5_b_im,
           s5_c_re, s5_c_im, s5_d, w_glu, g_kv, w_kv, w_fgate, b_fgate, g_ffn, w_ffn_up, conv_w, conv_b,
           w_ffn_down, g_final):
    bsz, seq, _ = x.shape
    depth = w_in.shape[0]
    n_a = depth // 2
    memkv = _memkv(mem, g_mem, w_mem_kv)
    h = x
    kv = qb = kb = None
    for l in range(depth):
        if l < n_a:
            tok, qmem = _s5_glu(h, g_mix[l], w_in[l], s5_a_re[l], s5_a_im[l], s5_log_dt[l], s5_b_re[l],
                                s5_b_im[l], s5_c_re[l], s5_c_im[l], s5_d[l], w_glu[l])
        else:
            q_scale = jnp.where(jnp.arange(D_MODEL) < D_TOK, (HEAD_DIM ** -0.5) * LOG2E, 1.0).astype(F32)
            qmem = _rms_proj(h.reshape(bsz * seq, D_MODEL), g_mix[l], w_in[l] * q_scale[None, :])
            qmem = qmem.reshape(bsz, seq, D_MODEL)
            tok = _fox(qmem, kv, qb, kb)
        h = _mix_ffn(h, tok, qmem, memkv[l], w_out[l], g_ffn[l], w_ffn_up[l], conv_w[l], conv_b[l],
                     w_ffn_down[l], g_final, final_norm=(l == depth - 1))
        if l == n_a - 1:
            kv, qb, kb = _kv_fgate(h, g_kv, w_kv, w_fgate, b_fgate)
    return h
```

```python
import functools
import math

import jax
import jax.numpy as jnp
from jax import lax
from jax.experimental import pallas as pl
from jax.experimental.pallas import tpu as pltpu

F32 = jnp.float32
BF16 = jnp.bfloat16

D_MODEL = 1024
HEAD_DIM = 64
D_MEM = 256
N_MEM_HEADS = 4
MEM_TOKENS = 256
D_TOK = D_MODEL - D_MEM
S5_GROUP = 16
S5_GROUPS = D_TOK // S5_GROUP
S5_STATE = 64
N_FOX_HEADS = D_TOK // HEAD_DIM
D_FF = 2816
EPS = 1e-6
LOG2E = math.log2(math.e)

LANES = 128
SUBLANES = 8
MXU_DIM = 256

ROW_TILE = 512
S5_STEPS = MXU_DIM // SUBLANES
S5_SLABS = D_TOK // MXU_DIM
S5_SLAB_STATE = (MXU_DIM // S5_GROUP) * S5_STATE
S5_STATE_ALL = S5_GROUPS * S5_STATE
S5_STAGES = 2
SCAN_LANES = 512
FF_CHUNK = 256
N_FF_CHUNKS = D_FF // FF_CHUNK
FOX_TQ = 256
FOX_TK = 512
FOX_UNROLL = 8
FOX_UNROLL_DIAG = 4
GATE_COLS = 8
CUMSUM_BLOCK = MXU_DIM
VMEM_LIMIT = 56 * 1024 * 1024


def _rmsnorm(x, g):
    ms = jnp.mean(x * x, axis=-1, keepdims=True)
    return x * lax.rsqrt(ms + EPS) * g


def _const_spec(shape):
    zeros = (0,) * len(shape)
    return pl.BlockSpec(shape, lambda *_: zeros, pipeline_mode=pl.Buffered(1))


def _memkv_kernel(mem_ref, g_ref, w_ref, o_ref):
    mn = _rmsnorm(mem_ref[0], g_ref[...]).astype(BF16)
    for l in range(w_ref.shape[0]):
        o_ref[l, 0] = jnp.dot(mn, w_ref[l], preferred_element_type=F32).astype(BF16)


def _memkv(mem, g_mem, w_mem_kv):
    bsz = mem.shape[0]
    depth = w_mem_kv.shape[0]
    return pl.pallas_call(
        _memkv_kernel,
        grid=(bsz,),
        in_specs=[
            pl.BlockSpec((1, MEM_TOKENS, D_MODEL), lambda b: (b, 0, 0)),
            _const_spec((1, D_MODEL)),
            _const_spec((depth, D_MODEL, 2 * D_MEM)),
        ],
        out_specs=pl.BlockSpec((depth, 1, MEM_TOKENS, 2 * D_MEM), lambda b: (0, b, 0, 0)),
        out_shape=jax.ShapeDtypeStruct((depth, bsz, MEM_TOKENS, 2 * D_MEM), BF16),
        compiler_params=pltpu.CompilerParams(dimension_semantics=("arbitrary",)),
        name="memkv",
    )(mem, g_mem.reshape(1, D_MODEL), w_mem_kv.astype(BF16))


def _s5_discretise(a_re, a_im, log_dt, b_re, b_im, c_re, c_im):
    dt = jnp.exp(log_dt.astype(F32))[:, None]
    lam_re = jnp.minimum(a_re.astype(F32), -1e-4)
    lam_im = a_im.astype(F32)
    mag = jnp.exp(lam_re * dt)
    ph = lam_im * dt
    ab_re, ab_im = mag * jnp.cos(ph), mag * jnp.sin(ph)
    den = lam_re * lam_re + lam_im * lam_im
    z_re = ((ab_re - 1.0) * lam_re + ab_im * lam_im) / den
    z_im = (ab_im * lam_re - (ab_re - 1.0) * lam_im) / den
    br, bi = b_re.astype(F32), b_im.astype(F32)
    bb_re = z_re[..., None] * br - z_im[..., None] * bi
    bb_im = z_re[..., None] * bi + z_im[..., None] * br
    gps = MXU_DIM // S5_GROUP
    eye = jnp.eye(gps, dtype=F32)

    def pack_b(bb):
        bb = bb.reshape(S5_SLABS, gps, S5_STATE, S5_GROUP)
        return jnp.einsum('sgpi,gh->sgihp', bb, eye).reshape(S5_SLABS, MXU_DIM, S5_SLAB_STATE)

    def pack_c(cc):
        cc = cc.reshape(S5_SLABS, gps, S5_GROUP, S5_STATE)
        return jnp.einsum('sgip,gh->sgphi', cc, eye).reshape(S5_SLABS, S5_SLAB_STATE, MXU_DIM)

    b_bd = jnp.concatenate([pack_b(bb_re), pack_b(bb_im)], axis=2).astype(BF16)
    c_bd = jnp.concatenate([pack_c(c_re.astype(F32)), -pack_c(c_im.astype(F32))], axis=1).astype(BF16)
    return ab_re.reshape(1, S5_STATE_ALL), ab_im.reshape(1, S5_STATE_ALL), b_bd, c_bd


def _s5_glu_kernel(x_ref, gmix_ref, win_ref, perm_ref, permt_ref, bbd_ref, cbd_ref, are_ref, aim_ref,
                   d_ref, wglu_ref, o_ref, qm_ref, hre_ref, him_ref, *bufs):
    rows = SUBLANES * S5_STEPS
    sre_refs, sim_refs, up_refs = bufs[0:S5_STAGES], bufs[S5_STAGES:2 * S5_STAGES], bufs[2 * S5_STAGES:]
    i = pl.program_id(0)

    @pl.when(i == 0)
    def _():
        hre_ref[...] = jnp.zeros_like(hre_ref)
        him_ref[...] = jnp.zeros_like(him_ref)
        for ref in bufs:
            ref[...] = jnp.zeros_like(ref)

    def permute_in(up_ref):
        xn = _rmsnorm(x_ref[...].reshape(rows, D_MODEL), gmix_ref[...]).astype(BF16)
        proj = jnp.dot(xn, win_ref[...], preferred_element_type=F32).astype(BF16)
        qm_ref[...] = proj[:, D_TOK:].reshape(SUBLANES, S5_STEPS, D_MEM)
        up_ref[...] = jnp.dot(perm_ref[...], proj[:, :D_TOK], preferred_element_type=F32).astype(BF16)

    def project_in(s, sre_ref, sim_ref, up_ref):
        bu = jnp.dot(up_ref[:, s * MXU_DIM:(s + 1) * MXU_DIM], bbd_ref[s], preferred_element_type=F32)
        sre_ref[:, s * S5_SLAB_STATE:(s + 1) * S5_SLAB_STATE] = bu[:, :S5_SLAB_STATE]
        sim_ref[:, s * S5_SLAB_STATE:(s + 1) * S5_SLAB_STATE] = bu[:, S5_SLAB_STATE:]

    def scan(c, sre_ref, sim_ref):
        cs = slice(c * SCAN_LANES, (c + 1) * SCAN_LANES)
        ar = jnp.broadcast_to(are_ref[:, cs], (SUBLANES, SCAN_LANES))
        ai = jnp.broadcast_to(aim_ref[:, cs], (SUBLANES, SCAN_LANES))
        hr = hre_ref[:, cs]
        hi = him_ref[:, cs]
        for t in range(S5_STEPS):
            rs = slice(t * SUBLANES, (t + 1) * SUBLANES)
            br = sre_ref[rs, cs]
            bi = sim_ref[rs, cs]
            hr, hi = ar * hr - ai * hi + br, ar * hi + ai * hr + bi
            sre_ref[rs, cs] = hr
            sim_ref[rs, cs] = hi
        hre_ref[:, cs] = hr
        him_ref[:, cs] = hi

    def project_out(s, sre_ref, sim_ref):
        ss = slice(s * S5_SLAB_STATE, (s + 1) * S5_SLAB_STATE)
        hcat = jnp.concatenate([sre_ref[:, ss], sim_ref[:, ss]], axis=1).astype(BF16)
        return jnp.dot(hcat, cbd_ref[s], preferred_element_type=F32)

    def gate_out(ys, up_ref):
        y = jnp.concatenate(ys, axis=1) + d_ref[...] * up_ref[...].astype(F32)
        g = jax.nn.gelu(y)
        z = jnp.dot(g.astype(BF16), wglu_ref[...], preferred_element_type=F32)
        tok = (g * jax.nn.sigmoid(z)).astype(BF16)
        out = jnp.dot(permt_ref[...], tok, preferred_element_type=F32).astype(BF16)
        o_ref[...] = out.reshape(SUBLANES, S5_STEPS, D_TOK)

    scans_per_slab = S5_STATE_ALL // SCAN_LANES // S5_SLABS
    for r in range(S5_STAGES):
        @pl.when(i % S5_STAGES == r)
        def _(r=r):
            a, b = r, (r - 1) % S5_STAGES
            permute_in(up_refs[a])
            ys = []
            for s in range(S5_SLABS):
                project_in(s, sre_refs[a], sim_refs[a], up_refs[a])
                for k in range(scans_per_slab):
                    scan(s * scans_per_slab + k, sre_refs[b], sim_refs[b])
                ys.append(project_out(s, sre_refs[b], sim_refs[b]))
            gate_out(ys, up_refs[b])


def _s5_glu(h, g_mix, w_in, a_re, a_im, log_dt, b_re, b_im, c_re, c_im, d_skip, w_glu):
    bsz, seq, _ = h.shape
    assert bsz == SUBLANES, "the scan keeps one batch element per sublane"
    rows = SUBLANES * S5_STEPS
    n_blocks = seq // S5_STEPS
    lag = S5_STAGES - 1
    ab_re, ab_im, b_bd, c_bd = _s5_discretise(a_re, a_im, log_dt, b_re, b_im, c_re, c_im)
    src = (jnp.arange(rows) % SUBLANES) * S5_STEPS + jnp.arange(rows) // SUBLANES
    perm = (src[:, None] == jnp.arange(rows)[None, :]).astype(BF16)
    stage_bufs = ([pltpu.VMEM((rows, S5_STATE_ALL), F32)] * (2 * S5_STAGES)
                  + [pltpu.VMEM((rows, D_TOK), BF16)] * S5_STAGES)
    return pl.pallas_call(
        _s5_glu_kernel,
        grid=(n_blocks + lag,),
        in_specs=[
            pl.BlockSpec((bsz, S5_STEPS, D_MODEL), lambda i: (0, jnp.minimum(i, n_blocks - 1), 0)),
            _const_spec((1, D_MODEL)),
            _const_spec((D_MODEL, D_MODEL)),
            _const_spec((rows, rows)),
            _const_spec((rows, rows)),
            _const_spec((S5_SLABS, MXU_DIM, 2 * S5_SLAB_STATE)),
            _const_spec((S5_SLABS, 2 * S5_SLAB_STATE, MXU_DIM)),
            _const_spec((1, S5_STATE_ALL)),
            _const_spec((1, S5_STATE_ALL)),
            _const_spec((1, D_TOK)),
            _const_spec((D_TOK, D_TOK)),
        ],
        out_specs=[
            pl.BlockSpec((bsz, S5_STEPS, D_TOK), lambda i: (0, jnp.maximum(i - lag, 0), 0)),
            pl.BlockSpec((bsz, S5_STEPS, D_MEM), lambda i: (0, jnp.minimum(i, n_blocks - 1), 0)),
        ],
        out_shape=[
            jax.ShapeDtypeStruct((bsz, seq, D_TOK), BF16),
            jax.ShapeDtypeStruct((bsz, seq, D_MEM), BF16),
        ],
        scratch_shapes=[
            pltpu.VMEM((SUBLANES, S5_STATE_ALL), F32),
            pltpu.VMEM((SUBLANES, S5_STATE_ALL), F32),
        ] + stage_bufs,
        compiler_params=pltpu.CompilerParams(dimension_semantics=("arbitrary",),
                                             vmem_limit_bytes=VMEM_LIMIT),
        name="s5_glu",
    )(h, g_mix.reshape(1, D_MODEL), w_in.astype(BF16), perm, perm.T, b_bd, c_bd, ab_re, ab_im,
      d_skip.reshape(1, D_TOK).astype(F32), w_glu.astype(BF16))


def _split3(x):
    hi = x.astype(BF16)
    r = x - hi.astype(F32)
    mid = r.astype(BF16)
    lo = (r - mid.astype(F32)).astype(BF16)
    return hi, mid, lo


def _gate_columns(z, tri_ref, carry_ref):
    n_rows = z.shape[0]
    logf = -(jnp.maximum(-z, 0.0) + jnp.log1p(jnp.exp(-jnp.abs(z))))
    tri = tri_ref[...]
    parts = jnp.concatenate(_split3(logf), axis=1)
    running = carry_ref[...]
    blocks = []
    for r in range(n_rows // CUMSUM_BLOCK):
        c3 = jnp.dot(tri, parts[r * CUMSUM_BLOCK:(r + 1) * CUMSUM_BLOCK], preferred_element_type=F32)
        csum = c3[:, :LANES] + c3[:, LANES:2 * LANES] + c3[:, 2 * LANES:]
        blocks.append(csum + running)
        running = blocks[-1][CUMSUM_BLOCK - 1:CUMSUM_BLOCK, :]
    carry_ref[...] = running
    fcum = jnp.concatenate(blocks, axis=0)
    hi, mid, lo = (part.astype(F32) for part in _split3(fcum * LOG2E))
    lane = lax.broadcasted_iota(jnp.int32, fcum.shape, 1)
    j = lane & (GATE_COLS - 1)
    valid = lane < N_FOX_HEADS * GATE_COLS
    one = jnp.ones_like(hi)
    zero = jnp.zeros_like(hi)
    qb = jnp.where(j == 0, hi, jnp.where(j == 1, mid, jnp.where(j == 2, lo, jnp.where(j < 6, one, zero))))
    kb = jnp.where(j < 3, one, jnp.where(j == 3, -hi, jnp.where(j == 4, -mid, jnp.where(j == 5, -lo, zero))))
    return jnp.where(valid, qb, zero).astype(BF16), jnp.where(valid, kb, zero).astype(BF16)


def _gate_operands(w_fgate, b_fgate):
    pad = LANES - N_FOX_HEADS * GATE_COLS
    wf = jnp.pad(jnp.repeat(w_fgate, GATE_COLS, axis=1), ((0, 0), (0, pad))).astype(BF16)
    bfr = jnp.pad(jnp.repeat(b_fgate, GATE_COLS), (0, pad)).reshape(1, LANES).astype(F32)
    tri = (jnp.arange(CUMSUM_BLOCK)[:, None] >= jnp.arange(CUMSUM_BLOCK)[None, :]).astype(BF16)
    return wf, bfr, tri


def _fox_kernel(itab_ref, jtab_ref, q_ref, qb_ref, k_ref, v_ref, kb_ref, o_ref,
                qq_ref, m_ref, acc_ref, s_ref, p_ref, a_ref, *, n_steps, n_diag):
    hp = pl.program_id(1)
    seq = q_ref.shape[1]
    n_q = seq // FOX_TQ
    rows = 2 * FOX_TQ
    n_lane_blocks = FOX_TK // LANES
    ones_cols = jnp.ones((FOX_TK, LANES), BF16)

    def scores(i, j):
        c0 = pl.multiple_of(j * FOX_TK, FOX_TK)
        kk = jnp.concatenate([k_ref[0, pl.ds(c0, FOX_TK), :], kb_ref[0, pl.ds(c0, FOX_TK), :]], axis=1)
        return lax.dot_general(qq_ref[i], kk, (((1,), (1,)), ((), ())), preferred_element_type=F32)

    def lane_blocks(s):
        return [s[:, n * LANES:(n + 1) * LANES] for n in range(n_lane_blocks)]

    def row_max(blocks):
        mx = functools.reduce(jnp.maximum, blocks)
        return jnp.broadcast_to(jnp.max(mx, axis=1, keepdims=True), (rows, LANES))

    def probs(blocks, m):
        return jnp.concatenate([jnp.exp2(blk - m).astype(BF16) for blk in blocks], axis=1)

    def pv_dot(p, j):
        c0 = pl.multiple_of(j * FOX_TK, FOX_TK)
        vv = jnp.concatenate([v_ref[0, pl.ds(c0, FOX_TK), :], ones_cols], axis=1)
        return jnp.dot(p, vv, preferred_element_type=F32)

    lane = lax.broadcasted_iota(jnp.int32, (FOX_TQ, LANES), 1)
    first = lane < HEAD_DIM
    gate_head = lane // GATE_COLS

    def build_q(i, carry):
        r0 = pl.multiple_of(i * FOX_TQ, FOX_TQ)
        q2 = q_ref[0, pl.ds(r0, FOX_TQ), :].astype(F32)
        qb = qb_ref[0, pl.ds(r0, FOX_TQ), :].astype(F32)
        zero = jnp.zeros_like(q2)
        qa = jnp.concatenate([jnp.where(first, q2, zero), jnp.where(gate_head == 2 * hp, qb, zero)], axis=1)
        qo = jnp.concatenate([jnp.where(first, zero, q2), jnp.where(gate_head == 2 * hp + 1, qb, zero)], axis=1)
        qq_ref[i] = jnp.concatenate([qa, qo], axis=0).astype(BF16)
        m_ref[i] = jnp.full((rows, LANES), -jnp.inf, F32)
        acc_ref[i] = jnp.zeros((rows, 2 * LANES), F32)
        return carry

    lax.fori_loop(0, n_q, build_q, 0)

    row = lax.broadcasted_iota(jnp.int32, (rows, FOX_TK), 0) & (FOX_TQ - 1)
    col = lax.broadcasted_iota(jnp.int32, (rows, FOX_TK), 1)

    def stage_scores(t):
        s_ref[t % 2] = scores(itab_ref[t], jtab_ref[t])

    def stage_softmax(t, masked):
        slot = t % 2
        i = itab_ref[t]
        s = s_ref[slot]
        if masked:
            s = jnp.where(i * FOX_TQ + row >= jtab_ref[t] * FOX_TK + col, s, -jnp.inf)
        blocks = lane_blocks(s)
        m_old = m_ref[i]
        m_new = jnp.maximum(m_old, row_max(blocks))
        m_ref[i] = m_new
        a_ref[slot] = jnp.exp2(m_old - m_new)
        p_ref[slot] = probs(blocks, m_new)

    def stage_pv(t):
        slot = t % 2
        i = itab_ref[t]
        alpha = a_ref[slot]
        acc_ref[i] = jnp.concatenate([alpha, alpha], axis=1) * acc_ref[i] + pv_dot(p_ref[slot], jtab_ref[t])

    def tick(t, parity, masked):
        assert parity in (0, 1)
        s_slot, p_slot = parity, 1 - parity
        s_ref[s_slot] = scores(itab_ref[t + 2], jtab_ref[t + 2])
        i1 = itab_ref[t + 1]
        s = s_ref[p_slot]
        if masked:
            s = jnp.where(i1 * FOX_TQ + row >= jtab_ref[t + 1] * FOX_TK + col, s, -jnp.inf)
        blocks = lane_blocks(s)
        i0 = itab_ref[t]
        alpha = a_ref[s_slot]
        acc_ref[i0] = (jnp.concatenate([alpha, alpha], axis=1) * acc_ref[i0]
                       + pv_dot(p_ref[s_slot], jtab_ref[t]))
        m_old = m_ref[i1]
        m_new = jnp.maximum(m_old, row_max(blocks))
        m_ref[i1] = m_new
        a_ref[p_slot] = jnp.exp2(m_old - m_new)
        p_ref[p_slot] = probs(blocks, m_new)

    def ticks(first_t, count, unroll, masked):
        assert count % unroll == 0 and unroll % 2 == 0 and first_t % 2 == 0

        def body(u, carry):
            for k in range(unroll):
                tick(first_t + unroll * u + k, k % 2, masked)
            return carry

        lax.fori_loop(0, count // unroll, body, 0)

    n_masked = n_diag - 2
    n_plain = n_steps - n_diag - 4
    assert n_diag % 2 == 0 and n_masked >= FOX_UNROLL_DIAG and n_plain >= FOX_UNROLL
    assert (n_masked % FOX_UNROLL_DIAG) % 2 == 0 and (n_plain % FOX_UNROLL) % 2 == 0
    stage_scores(0)
    stage_scores(1)
    stage_softmax(0, True)
    main = n_masked - n_masked % FOX_UNROLL_DIAG
    ticks(0, main, FOX_UNROLL_DIAG, True)
    if n_masked > main:
        ticks(main, n_masked - main, n_masked - main, True)
    tick(n_diag - 2, 0, True)
    tick(n_diag - 1, 1, False)
    main = n_plain - n_plain % FOX_UNROLL
    ticks(n_diag, main, FOX_UNROLL, False)
    if n_plain > main:
        ticks(n_diag + main, n_plain - main, n_plain - main, False)
    tick(n_steps - 4, 0, False)
    tick(n_steps - 3, 1, False)
    stage_softmax(n_steps - 1, False)
    stage_pv(n_steps - 2)
    stage_pv(n_steps - 1)

    def finish(i, carry):
        acc = acc_ref[i]
        o = acc[:, :LANES] / acc[:, LANES:]
        r0 = pl.multiple_of(i * FOX_TQ, FOX_TQ)
        o_ref[0, pl.ds(r0, FOX_TQ), :] = jnp.where(first, o[:FOX_TQ], o[FOX_TQ:]).astype(BF16)
        return carry

    lax.fori_loop(0, n_q, finish, 0)


def _fox(proj, kv, qb, kb):
    bsz, seq, _ = proj.shape
    n_pairs = D_TOK // LANES
    n_q = seq // FOX_TQ
    rows = 2 * FOX_TQ
    diag = [(i, (i * FOX_TQ) // FOX_TK) for i in range(n_q)]
    pairs = diag + [(i, j) for j in range(seq // FOX_TK) for i in range(n_q) if (i * FOX_TQ) // FOX_TK > j]
    itab = jnp.asarray([p[0] for p in pairs], jnp.int32)
    jtab = jnp.asarray([p[1] for p in pairs], jnp.int32)
    grid_spec = pltpu.PrefetchScalarGridSpec(
        num_scalar_prefetch=2,
        grid=(bsz, n_pairs),
        in_specs=[
            pl.BlockSpec((1, seq, LANES), lambda b, p, *_: (b, 0, p)),
            pl.BlockSpec((1, seq, LANES), lambda b, p, *_: (b, 0, 0)),
            pl.BlockSpec((1, seq, LANES), lambda b, p, *_: (b, 0, p)),
            pl.BlockSpec((1, seq, LANES), lambda b, p, *_: (b, 0, n_pairs + p)),
            pl.BlockSpec((1, seq, LANES), lambda b, p, *_: (b, 0, 0)),
        ],
        out_specs=pl.BlockSpec((1, seq, LANES), lambda b, p, *_: (b, 0, p)),
        scratch_shapes=[
            pltpu.VMEM((n_q, rows, 2 * LANES), BF16),
            pltpu.VMEM((n_q, rows, LANES), F32),
            pltpu.VMEM((n_q, rows, 2 * LANES), F32),
            pltpu.VMEM((2, rows, FOX_TK), F32),
            pltpu.VMEM((2, rows, FOX_TK), BF16),
            pltpu.VMEM((2, rows, LANES), F32),
        ],
    )
    return pl.pallas_call(
        functools.partial(_fox_kernel, n_steps=len(pairs), n_diag=len(diag)),
        grid_spec=grid_spec,
        out_shape=jax.ShapeDtypeStruct((bsz, seq, D_TOK), BF16),
        compiler_params=pltpu.CompilerParams(dimension_semantics=("arbitrary", "arbitrary"),
                                             vmem_limit_bytes=VMEM_LIMIT),
        name="fox",
    )(itab, jtab, proj, qb, kv, kv, kb)


def _mix_ffn_kernel(*refs, final_norm, emit_next):
    (h_ref, tok_ref, qm_ref, mkv_ref, wout_ref, gffn_ref, wup_ref, cw_ref, wdn_ref, gfin_ref), refs = (
        refs[:10], refs[10:])
    if emit_next:
        (gmix_ref, win_ref, gkv_ref, wkvf_ref, bf_ref, tri_ref), refs = refs[:6], refs[6:]
        (o_ref, proj_ref, kv_ref, qb_ref, kb_ref), refs = refs[:5], refs[5:]
        carry_ref, acc_ref, xn_ref, up_ref, act_ref, fcarry_ref = refs
    else:
        o_ref, carry_ref, acc_ref, xn_ref, up_ref, act_ref = refs
    tm = h_ref.shape[1]

    qm = qm_ref[0].astype(F32) * ((D_MEM // N_MEM_HEADS) ** -0.5)
    km = mkv_ref[0, :, :D_MEM]
    vm = mkv_ref[0, :, D_MEM:]
    lane = lax.broadcasted_iota(jnp.int32, (tm, D_MEM), 1)
    mem = jnp.zeros((tm, D_MEM), F32)
    for hh in range(N_MEM_HEADS):
        sel = (lane >= hh * HEAD_DIM) & (lane < (hh + 1) * HEAD_DIM)
        qh = jnp.where(sel, qm, jnp.zeros_like(qm)).astype(BF16)
        s = lax.dot_general(qh, km, (((1,), (1,)), ((), ())), preferred_element_type=F32)
        p = jnp.exp(s - jnp.max(s, axis=1, keepdims=True))
        o = jnp.dot(p.astype(BF16), vm, preferred_element_type=F32)
        mem = jnp.where(sel, o / jnp.sum(p, axis=1, keepdims=True), mem)

    cat = jnp.concatenate([tok_ref[0], mem.astype(BF16)], axis=1)
    hmid = h_ref[0] + jnp.dot(cat, wout_ref[...], preferred_element_type=F32)
    acc_ref[...] = hmid
    xn = _rmsnorm(hmid, gffn_ref[...]).astype(BF16)

    @pl.when(pl.program_id(1) == 0)
    def _():
        carry_ref[...] = jnp.zeros_like(carry_ref)
        if emit_next:
            fcarry_ref[...] = jnp.zeros_like(fcarry_ref)

    xn_ref[...] = xn
    row8 = lax.broadcasted_iota(jnp.int32, (SUBLANES, FF_CHUNK), 0)
    def stage_up(c):
        for half in range(2):
            cols = slice(half * D_FF + c * FF_CHUNK, half * D_FF + (c + 1) * FF_CHUNK)
            up_ref[c % 2, :, half * FF_CHUNK:(half + 1) * FF_CHUNK] = jnp.dot(
                xn_ref[...], wup_ref[:, cols], preferred_element_type=F32)

    stage_up(0)
    for c in range(N_FF_CHUNKS):
        if c + 1 < N_FF_CHUNKS:
            stage_up(c + 1)
        a = up_ref[c % 2, :, :FF_CHUNK]
        g = up_ref[c % 2, :, FF_CHUNK:]
        prev = carry_ref[c]
        carry_ref[c] = g[tm - SUBLANES:, :]
        g1 = pltpu.roll(g, 1, 0)
        g2 = pltpu.roll(g, 2, 0)
        g1 = jnp.concatenate([jnp.where(row8 < 1, pltpu.roll(prev, 1, 0), g1[:SUBLANES]), g1[SUBLANES:]], axis=0)
        g2 = jnp.concatenate([jnp.where(row8 < 2, pltpu.roll(prev, 2, 0), g2[:SUBLANES]), g2[SUBLANES:]], axis=0)
        cw = cw_ref[:, c * FF_CHUNK:(c + 1) * FF_CHUNK]
        gc = g2 * cw[0:1] + g1 * cw[1:2] + g * cw[2:3] + cw[3:4]
        act_ref[:, c * FF_CHUNK:(c + 1) * FF_CHUNK] = (gc * jax.nn.sigmoid(gc) * a).astype(BF16)

    out = acc_ref[...] + jnp.dot(act_ref[...], wdn_ref[...], preferred_element_type=F32)
    if final_norm:
        out = _rmsnorm(out, gfin_ref[...])
    o_ref[0] = out
    if emit_next:
        normed = out * lax.rsqrt(jnp.mean(out * out, axis=-1, keepdims=True) + EPS)
        xn_next = (normed * gmix_ref[...]).astype(BF16)
        proj_ref[0] = jnp.dot(xn_next, win_ref[...], preferred_element_type=F32).astype(BF16)
        hs = (normed * gkv_ref[...]).astype(BF16)
        kvz = jnp.dot(hs, wkvf_ref[...], preferred_element_type=F32)
        kv_ref[0] = kvz[:, :2 * D_TOK].astype(BF16)
        qb_ref[0], kb_ref[0] = _gate_columns(kvz[:, 2 * D_TOK:] + bf_ref[...], tri_ref, fcarry_ref)


def _mix_ffn(h, tok, qmem, memkv, w_out, g_ffn, w_up, conv_w, conv_b, w_down, g_final, final_norm,
             next_layer=None):
    bsz, seq, _ = h.shape
    qmem_block = qmem.shape[-1] // D_MEM - 1
    tm = ROW_TILE
    wup = w_up.astype(BF16)
    wdn = w_down.astype(BF16)
    cw = jnp.concatenate([conv_w, conv_b[None, :], jnp.zeros((SUBLANES - 4, D_FF), conv_w.dtype)],
                         axis=0).astype(F32)
    def row_spec(width):
        return pl.BlockSpec((1, tm, width), lambda b, t: (b, t, 0))

    operands = [h, tok, qmem, memkv, w_out.astype(BF16), g_ffn.reshape(1, D_MODEL), wup, cw, wdn,
                g_final.reshape(1, D_MODEL)]
    in_specs = [
        row_spec(D_MODEL),
        row_spec(D_TOK),
        pl.BlockSpec((1, tm, D_MEM), lambda b, t: (b, t, qmem_block)),
        pl.BlockSpec((1, MEM_TOKENS, 2 * D_MEM), lambda b, t: (b, 0, 0)),
        _const_spec((D_MODEL, D_MODEL)),
        _const_spec((1, D_MODEL)),
        _const_spec((D_MODEL, 2 * D_FF)),
        _const_spec((SUBLANES, D_FF)),
        _const_spec((D_FF, D_MODEL)),
        _const_spec((1, D_MODEL)),
    ]
    out_specs = [row_spec(D_MODEL)]
    out_shape = [jax.ShapeDtypeStruct((bsz, seq, D_MODEL), F32)]
    scratch_shapes = [
        pltpu.VMEM((N_FF_CHUNKS, SUBLANES, FF_CHUNK), F32),
        pltpu.VMEM((tm, D_MODEL), F32),
        pltpu.VMEM((tm, D_MODEL), BF16),
        pltpu.VMEM((2, tm, 2 * FF_CHUNK), F32),
        pltpu.VMEM((tm, D_FF), BF16),
    ]
    if next_layer is not None:
        g_mix_n, w_in_n, g_kv, w_kv, w_fgate, b_fgate = next_layer
        wf, bfr, tri = _gate_operands(w_fgate, b_fgate)
        wkvf = jnp.concatenate([w_kv.astype(BF16), wf], axis=1)
        operands += [g_mix_n.reshape(1, D_MODEL), w_in_n.astype(BF16), g_kv.reshape(1, D_MODEL),
                     wkvf, bfr, tri]
        in_specs += [_const_spec((1, D_MODEL)), _const_spec((D_MODEL, D_MODEL)), _const_spec((1, D_MODEL)),
                     _const_spec((D_MODEL, 2 * D_TOK + LANES)), _const_spec((1, LANES)),
                     _const_spec((CUMSUM_BLOCK, CUMSUM_BLOCK))]
        out_specs += [row_spec(D_MODEL), row_spec(2 * D_TOK), row_spec(LANES), row_spec(LANES)]
        out_shape += [jax.ShapeDtypeStruct((bsz, seq, n), BF16) for n in (D_MODEL, 2 * D_TOK, LANES, LANES)]
        scratch_shapes.append(pltpu.VMEM((1, LANES), F32))
    outs = pl.pallas_call(
        functools.partial(_mix_ffn_kernel, final_norm=final_norm, emit_next=next_layer is not None),
        grid=(bsz, seq // tm),
        in_specs=in_specs,
        out_specs=out_specs,
        out_shape=out_shape,
        scratch_shapes=scratch_shapes,
        compiler_params=pltpu.CompilerParams(dimension_semantics=("arbitrary", "arbitrary"),
                                             vmem_limit_bytes=VMEM_LIMIT),
        name="mix_ffn",
    )(*operands)
    return outs if next_layer is not None else outs[0]


def kernel(x, mem, g_mix, w_in, w_out, g_mem, w_mem_kv, s5_a_re, s5_a_im, s5_log_dt, s5_b_re, s5_b_im,
           s5_c_re, s5_c_im, s5_d, w_glu, g_kv, w_kv, w_fgate, b_fgate, g_ffn, w_ffn_up, conv_w, conv_b,
           w_ffn_down, g_final):
    depth = w_in.shape[0]
    assert depth == 2, "one S5 layer followed by one attention layer"
    memkv = _memkv(mem, g_mem, w_mem_kv)

    def ffn_args(l):
        return (memkv[l], w_out[l], g_ffn[l], w_ffn_up[l], conv_w[l], conv_b[l], w_ffn_down[l], g_final)

    tok, qmem = _s5_glu(x, g_mix[0], w_in[0], s5_a_re[0], s5_a_im[0], s5_log_dt[0], s5_b_re[0],
                        s5_b_im[0], s5_c_re[0], s5_c_im[0], s5_d[0], w_glu[0])
    q_scale = jnp.where(jnp.arange(D_MODEL) < D_TOK, (HEAD_DIM ** -0.5) * LOG2E, 1.0).astype(F32)
    h, proj, kv, qb, kb = _mix_ffn(x, tok, qmem, *ffn_args(0), final_norm=False,
                                   next_layer=(g_mix[1], w_in[1] * q_scale[None, :], g_kv, w_kv,
                                               w_fgate, b_fgate))
    tok = _fox(proj, kv, qb, kb)
    return _mix_ffn(h, tok, proj, *ffn_args(1), final_norm=True)
```

```python
import functools
import math

import jax
import jax.numpy as jnp
from jax import lax
from jax.experimental import pallas as pl
from jax.experimental.pallas import tpu as pltpu

F32 = jnp.float32
BF16 = jnp.bfloat16

D_MODEL = 1024
HEAD_DIM = 64
D_MEM = 256
N_MEM_HEADS = 4
MEM_TOKENS = 256
D_TOK = D_MODEL - D_MEM
S5_GROUP = 16
S5_GROUPS = D_TOK // S5_GROUP
S5_STATE = 64
N_FOX_HEADS = D_TOK // HEAD_DIM
D_FF = 2816
EPS = 1e-6
LOG2E = math.log2(math.e)

LANES = 128
SUBLANES = 8
MXU_DIM = 256

ROW_TILE = 512
S5_STEPS = MXU_DIM // SUBLANES
S5_SLABS = D_TOK // MXU_DIM
S5_SLAB_STATE = (MXU_DIM // S5_GROUP) * S5_STATE
S5_STATE_ALL = S5_GROUPS * S5_STATE
S5_STAGES = 2
SCAN_LANES = 512
FF_CHUNK = 256
N_FF_CHUNKS = D_FF // FF_CHUNK
FOX_TQ = 256
FOX_TK = 512
FOX_UNROLL = 8
FOX_UNROLL_DIAG = 4
GATE_COLS = 8
CUMSUM_BLOCK = MXU_DIM
VMEM_LIMIT = 56 * 1024 * 1024


def _rmsnorm(x, g):
    ms = jnp.mean(x * x, axis=-1, keepdims=True)
    return x * lax.rsqrt(ms + EPS) * g


def _const_spec(shape):
    zeros = (0,) * len(shape)
    return pl.BlockSpec(shape, lambda *_: zeros, pipeline_mode=pl.Buffered(1))


def _memkv_kernel(mem_ref, g_ref, w_ref, o_ref):
    mn = _rmsnorm(mem_ref[0], g_ref[...]).astype(BF16)
    for l in range(w_ref.shape[0]):
        o_ref[l, 0] = jnp.dot(mn, w_ref[l], preferred_element_type=F32).astype(BF16)


def _memkv(mem, g_mem, w_mem_kv):
    bsz = mem.shape[0]
    depth = w_mem_kv.shape[0]
    return pl.pallas_call(
        _memkv_kernel,
        grid=(bsz,),
        in_specs=[
            pl.BlockSpec((1, MEM_TOKENS, D_MODEL), lambda b: (b, 0, 0)),
            _const_spec((1, D_MODEL)),
            _const_spec((depth, D_MODEL, 2 * D_MEM)),
        ],
        out_specs=pl.BlockSpec((depth, 1, MEM_TOKENS, 2 * D_MEM), lambda b: (0, b, 0, 0)),
        out_shape=jax.ShapeDtypeStruct((depth, bsz, MEM_TOKENS, 2 * D_MEM), BF16),
        compiler_params=pltpu.CompilerParams(dimension_semantics=("arbitrary",)),
        name="memkv",
    )(mem, g_mem.reshape(1, D_MODEL), w_mem_kv.astype(BF16))


def _s5_discretise(a_re, a_im, log_dt, b_re, b_im, c_re, c_im):
    dt = jnp.exp(log_dt.astype(F32))[:, None]
    lam_re = jnp.minimum(a_re.astype(F32), -1e-4)
    lam_im = a_im.astype(F32)
    mag = jnp.exp(lam_re * dt)
    ph = lam_im * dt
    ab_re, ab_im = mag * jnp.cos(ph), mag * jnp.sin(ph)
    den = lam_re * lam_re + lam_im * lam_im
    z_re = ((ab_re - 1.0) * lam_re + ab_im * lam_im) / den
    z_im = (ab_im * lam_re - (ab_re - 1.0) * lam_im) / den
    br, bi = b_re.astype(F32), b_im.astype(F32)
    bb_re = z_re[..., None] * br - z_im[..., None] * bi
    bb_im = z_re[..., None] * bi + z_im[..., None] * br
    gps = MXU_DIM // S5_GROUP
    eye = jnp.eye(gps, dtype=F32)

    def pack_b(bb):
        bb = bb.reshape(S5_SLABS, gps, S5_STATE, S5_GROUP)
        return jnp.einsum('sgpi,gh->sgihp', bb, eye).reshape(S5_SLABS, MXU_DIM, S5_SLAB_STATE)

    def pack_c(cc):
        cc = cc.reshape(S5_SLABS, gps, S5_GROUP, S5_STATE)
        return jnp.einsum('sgip,gh->sgphi', cc, eye).reshape(S5_SLABS, S5_SLAB_STATE, MXU_DIM)

    b_bd = jnp.concatenate([pack_b(bb_re), pack_b(bb_im)], axis=2).astype(BF16)
    c_bd = jnp.concatenate([pack_c(c_re.astype(F32)), -pack_c(c_im.astype(F32))], axis=1).astype(BF16)
    return ab_re.reshape(1, S5_STATE_ALL), ab_im.reshape(1, S5_STATE_ALL), b_bd, c_bd


def _s5_glu_kernel(x_ref, gmix_ref, win_ref, perm_ref, permt_ref, bbd_ref, cbd_ref, are_ref, aim_ref,
                   d_ref, wglu_ref, o_ref, qm_ref, hre_ref, him_ref, *bufs):
    rows = SUBLANES * S5_STEPS
    sre_refs, sim_refs, up_refs = bufs[0:S5_STAGES], bufs[S5_STAGES:2 * S5_STAGES], bufs[2 * S5_STAGES:]
    i = pl.program_id(0)

    @pl.when(i == 0)
    def _():
        hre_ref[...] = jnp.zeros_like(hre_ref)
        him_ref[...] = jnp.zeros_like(him_ref)
        for ref in bufs:
            ref[...] = jnp.zeros_like(ref)

    def permute_in(up_ref):
        xn = _rmsnorm(x_ref[...].reshape(rows, D_MODEL), gmix_ref[...]).astype(BF16)
        proj = jnp.dot(xn, win_ref[...], preferred_element_type=F32).astype(BF16)
        qm_ref[...] = proj[:, D_TOK:].reshape(SUBLANES, S5_STEPS, D_MEM)
        up_ref[...] = jnp.dot(perm_ref[...], proj[:, :D_TOK], preferred_element_type=F32).astype(BF16)

    def project_in(s, sre_ref, sim_ref, up_ref):
        bu = jnp.dot(up_ref[:, s * MXU_DIM:(s + 1) * MXU_DIM], bbd_ref[s], preferred_element_type=F32)
        sre_ref[:, s * S5_SLAB_STATE:(s + 1) * S5_SLAB_STATE] = bu[:, :S5_SLAB_STATE]
        sim_ref[:, s * S5_SLAB_STATE:(s + 1) * S5_SLAB_STATE] = bu[:, S5_SLAB_STATE:]

    def scan(c, sre_ref, sim_ref):
        cs = slice(c * SCAN_LANES, (c + 1) * SCAN_LANES)
        ar = jnp.broadcast_to(are_ref[:, cs], (SUBLANES, SCAN_LANES))
        ai = jnp.broadcast_to(aim_ref[:, cs], (SUBLANES, SCAN_LANES))
        hr = hre_ref[:, cs]
        hi = him_ref[:, cs]
        for t in range(S5_STEPS):
            rs = slice(t * SUBLANES, (t + 1) * SUBLANES)
            br = sre_ref[rs, cs]
            bi = sim_ref[rs, cs]
            hr, hi = ar * hr - ai * hi + br, ar * hi + ai * hr + bi
            sre_ref[rs, cs] = hr
            sim_ref[rs, cs] = hi
        hre_ref[:, cs] = hr
        him_ref[:, cs] = hi

    def project_out(s, sre_ref, sim_ref):
        ss = slice(s * S5_SLAB_STATE, (s + 1) * S5_SLAB_STATE)
        y_re = jnp.dot(sre_ref[:, ss].astype(BF16), cbd_ref[s, :S5_SLAB_STATE], preferred_element_type=F32)
        y_im = jnp.dot(sim_ref[:, ss].astype(BF16), cbd_ref[s, S5_SLAB_STATE:], preferred_element_type=F32)
        return y_re + y_im

    def gate_out(ys, up_ref):
        y = jnp.concatenate(ys, axis=1) + d_ref[...] * up_ref[...].astype(F32)
        toks = []
        for half in range(2):
            g = jax.nn.gelu(y[half * rows // 2:(half + 1) * rows // 2])
            z = jnp.dot(g.astype(BF16), wglu_ref[...], preferred_element_type=F32)
            toks.append((g * jax.nn.sigmoid(z)).astype(BF16))
        tok = jnp.concatenate(toks, axis=0)
        out = jnp.dot(permt_ref[...], tok, preferred_element_type=F32).astype(BF16)
        o_ref[...] = out.reshape(SUBLANES, S5_STEPS, D_TOK)

    scans_per_slab = S5_STATE_ALL // SCAN_LANES // S5_SLABS
    for r in range(S5_STAGES):
        @pl.when(i % S5_STAGES == r)
        def _(r=r):
            a, b = r, (r - 1) % S5_STAGES
            permute_in(up_refs[a])
            ys = []
            for s in range(S5_SLABS):
                project_in(s, sre_refs[a], sim_refs[a], up_refs[a])
                for k in range(scans_per_slab):
                    scan(s * scans_per_slab + k, sre_refs[b], sim_refs[b])
                ys.append(project_out(s, sre_refs[b], sim_refs[b]))
            gate_out(ys, up_refs[b])


def _s5_glu(h, g_mix, w_in, a_re, a_im, log_dt, b_re, b_im, c_re, c_im, d_skip, w_glu):
    bsz, seq, _ = h.shape
    assert bsz == SUBLANES, "the scan keeps one batch element per sublane"
    rows = SUBLANES * S5_STEPS
    n_blocks = seq // S5_STEPS
    lag = S5_STAGES - 1
    ab_re, ab_im, b_bd, c_bd = _s5_discretise(a_re, a_im, log_dt, b_re, b_im, c_re, c_im)
    src = (jnp.arange(rows) % SUBLANES) * S5_STEPS + jnp.arange(rows) // SUBLANES
    perm = (src[:, None] == jnp.arange(rows)[None, :]).astype(BF16)
    stage_bufs = ([pltpu.VMEM((rows, S5_STATE_ALL), F32)] * (2 * S5_STAGES)
                  + [pltpu.VMEM((rows, D_TOK), BF16)] * S5_STAGES)
    return pl.pallas_call(
        _s5_glu_kernel,
        grid=(n_blocks + lag,),
        in_specs=[
            pl.BlockSpec((bsz, S5_STEPS, D_MODEL), lambda i: (0, jnp.minimum(i, n_blocks - 1), 0)),
            _const_spec((1, D_MODEL)),
            _const_spec((D_MODEL, D_MODEL)),
            _const_spec((rows, rows)),
            _const_spec((rows, rows)),
            _const_spec((S5_SLABS, MXU_DIM, 2 * S5_SLAB_STATE)),
            _const_spec((S5_SLABS, 2 * S5_SLAB_STATE, MXU_DIM)),
            _const_spec((1, S5_STATE_ALL)),
            _const_spec((1, S5_STATE_ALL)),
            _const_spec((1, D_TOK)),
            _const_spec((D_TOK, D_TOK)),
        ],
        out_specs=[
            pl.BlockSpec((bsz, S5_STEPS, D_TOK), lambda i: (0, jnp.maximum(i - lag, 0), 0)),
            pl.BlockSpec((bsz, S5_STEPS, D_MEM), lambda i: (0, jnp.minimum(i, n_blocks - 1), 0)),
        ],
        out_shape=[
            jax.ShapeDtypeStruct((bsz, seq, D_TOK), BF16),
            jax.ShapeDtypeStruct((bsz, seq, D_MEM), BF16),
        ],
        scratch_shapes=[
            pltpu.VMEM((SUBLANES, S5_STATE_ALL), F32),
            pltpu.VMEM((SUBLANES, S5_STATE_ALL), F32),
        ] + stage_bufs,
        compiler_params=pltpu.CompilerParams(dimension_semantics=("arbitrary",),
                                             vmem_limit_bytes=VMEM_LIMIT),
        name="s5_glu",
    )(h, g_mix.reshape(1, D_MODEL), w_in.astype(BF16), perm, perm.T, b_bd, c_bd, ab_re, ab_im,
      d_skip.reshape(1, D_TOK).astype(F32), w_glu.astype(BF16))


def _split3(x):
    hi = x.astype(BF16)
    r = x - hi.astype(F32)
    mid = r.astype(BF16)
    lo = (r - mid.astype(F32)).astype(BF16)
    return hi, mid, lo


def _gate_columns(z, tri_ref, carry_ref):
    n_rows = z.shape[0]
    logf = -(jnp.maximum(-z, 0.0) + jnp.log1p(jnp.exp(-jnp.abs(z))))
    tri = tri_ref[...]
    parts = jnp.concatenate(_split3(logf), axis=1)
    running = carry_ref[...]
    blocks = []
    for r in range(n_rows // CUMSUM_BLOCK):
        c3 = jnp.dot(tri, parts[r * CUMSUM_BLOCK:(r + 1) * CUMSUM_BLOCK], preferred_element_type=F32)
        csum = c3[:, :LANES] + c3[:, LANES:2 * LANES] + c3[:, 2 * LANES:]
        blocks.append(csum + running)
        running = blocks[-1][CUMSUM_BLOCK - 1:CUMSUM_BLOCK, :]
    carry_ref[...] = running
    fcum = jnp.concatenate(blocks, axis=0)
    hi, mid, lo = (part.astype(F32) for part in _split3(fcum * LOG2E))
    lane = lax.broadcasted_iota(jnp.int32, fcum.shape, 1)
    j = lane & (GATE_COLS - 1)
    valid = lane < N_FOX_HEADS * GATE_COLS
    one = jnp.ones_like(hi)
    zero = jnp.zeros_like(hi)
    qb = jnp.where(j == 0, hi, jnp.where(j == 1, mid, jnp.where(j == 2, lo, jnp.where(j < 6, one, zero))))
    kb = jnp.where(j < 3, one, jnp.where(j == 3, -hi, jnp.where(j == 4, -mid, jnp.where(j == 5, -lo, zero))))
    return jnp.where(valid, qb, zero).astype(BF16), jnp.where(valid, kb, zero).astype(BF16)


def _gate_operands(w_fgate, b_fgate):
    pad = LANES - N_FOX_HEADS * GATE_COLS
    wf = jnp.pad(jnp.repeat(w_fgate, GATE_COLS, axis=1), ((0, 0), (0, pad))).astype(BF16)
    bfr = jnp.pad(jnp.repeat(b_fgate, GATE_COLS), (0, pad)).reshape(1, LANES).astype(F32)
    tri = (jnp.arange(CUMSUM_BLOCK)[:, None] >= jnp.arange(CUMSUM_BLOCK)[None, :]).astype(BF16)
    return wf, bfr, tri


def _fox_kernel(itab_ref, ktab_ref, q_ref, qb_ref, k_ref, v_ref, kb_ref, o_ref,
                qq_ref, m_ref, acc_ref, s_ref, p_ref, a_ref, *, kinds):
    hp = pl.program_id(1)
    seq = q_ref.shape[1]
    n_q = seq // FOX_TQ
    rows = 2 * FOX_TQ
    n_steps = len(kinds)

    def scores(t, kw):
        c0 = pl.multiple_of(ktab_ref[t], FOX_TQ)
        kk = jnp.concatenate([k_ref[0, pl.ds(c0, kw), :], kb_ref[0, pl.ds(c0, kw), :]], axis=1)
        return lax.dot_general(qq_ref[itab_ref[t]], kk, (((1,), (1,)), ((), ())), preferred_element_type=F32)

    def lane_blocks(s):
        return [s[:, n * LANES:(n + 1) * LANES] for n in range(s.shape[1] // LANES)]

    def row_max(blocks):
        mx = functools.reduce(jnp.maximum, blocks)
        return jnp.broadcast_to(jnp.max(mx, axis=1, keepdims=True), (rows, LANES))

    def probs(blocks, m):
        return jnp.concatenate([jnp.exp2(blk - m).astype(BF16) for blk in blocks], axis=1)

    def pv_dot(p, t, kw):
        c0 = pl.multiple_of(ktab_ref[t], FOX_TQ)
        vv = jnp.concatenate([v_ref[0, pl.ds(c0, kw), :], jnp.ones((kw, LANES), BF16)], axis=1)
        return jnp.dot(p, vv, preferred_element_type=F32)

    lane = lax.broadcasted_iota(jnp.int32, (FOX_TQ, LANES), 1)
    first = lane < HEAD_DIM
    gate_head = lane // GATE_COLS

    def build_q(i, carry):
        r0 = pl.multiple_of(i * FOX_TQ, FOX_TQ)
        q2 = q_ref[0, pl.ds(r0, FOX_TQ), :].astype(F32)
        qb = qb_ref[0, pl.ds(r0, FOX_TQ), :].astype(F32)
        zero = jnp.zeros_like(q2)
        qa = jnp.concatenate([jnp.where(first, q2, zero), jnp.where(gate_head == 2 * hp, qb, zero)], axis=1)
        qo = jnp.concatenate([jnp.where(first, zero, q2), jnp.where(gate_head == 2 * hp + 1, qb, zero)], axis=1)
        qq_ref[i] = jnp.concatenate([qa, qo], axis=0).astype(BF16)
        m_ref[i] = jnp.full((rows, LANES), -jnp.inf, F32)
        acc_ref[i] = jnp.zeros((rows, 2 * LANES), F32)
        return carry

    lax.fori_loop(0, n_q, build_q, 0)

    visible = ((lax.broadcasted_iota(jnp.int32, (rows, FOX_TQ), 0) & (FOX_TQ - 1))
               >= lax.broadcasted_iota(jnp.int32, (rows, FOX_TQ), 1))

    def stage_scores(t, slot, kind):
        kw, _ = kind
        s_ref[slot, :, :kw] = scores(t, kw)

    def load_blocks(slot, kind):
        kw, masked = kind
        s = s_ref[slot, :, :kw]
        if masked:
            assert kw == FOX_TQ
            s = jnp.where(visible, s, -jnp.inf)
        return lane_blocks(s)

    def softmax_update(t, slot, kind, blocks):
        kw, _ = kind
        i = itab_ref[t]
        m_old = m_ref[i]
        m_new = jnp.maximum(m_old, row_max(blocks))
        m_ref[i] = m_new
        a_ref[slot] = jnp.exp2(m_old - m_new)
        p_ref[slot, :, :kw] = probs(blocks, m_new)

    def stage_pv(t, slot, kind):
        kw, _ = kind
        i = itab_ref[t]
        alpha = a_ref[slot]
        acc_ref[i] = (jnp.concatenate([alpha, alpha], axis=1) * acc_ref[i]
                      + pv_dot(p_ref[slot, :, :kw], t, kw))

    def tick(t, parity, kind2, kind1, kind0):
        assert parity in (0, 1)
        stage_scores(t + 2, parity, kind2)
        blocks = load_blocks(1 - parity, kind1)
        stage_pv(t, parity, kind0)
        softmax_update(t + 1, 1 - parity, kind1, blocks)

    stage_scores(0, 0, kinds[0])
    stage_scores(1, 1, kinds[1])
    softmax_update(0, 0, kinds[0], load_blocks(0, kinds[0]))
    t = 0
    while t < n_steps - 2:
        trio = (kinds[t + 2], kinds[t + 1], kinds[t])
        run = 1
        while t + run < n_steps - 2 and (kinds[t + run + 2], kinds[t + run + 1], kinds[t + run]) == trio:
            run += 1
        unroll = FOX_UNROLL if trio[0][0] == FOX_TK else FOX_UNROLL_DIAG
        looped = (run // unroll) * unroll if t % 2 == 0 else 0
        if looped:
            def body(u, carry, t=t, unroll=unroll, trio=trio):
                for k in range(unroll):
                    tick(t + unroll * u + k, k % 2, *trio)
                return carry

            lax.fori_loop(0, looped // unroll, body, 0)
        for k in range(looped, run):
            tick(t + k, (t + k) % 2, *trio)
        t += run
    last = n_steps - 1
    softmax_update(last, last % 2, kinds[last], load_blocks(last % 2, kinds[last]))
    stage_pv(last - 1, (last - 1) % 2, kinds[last - 1])
    stage_pv(last, last % 2, kinds[last])

    def finish(i, carry):
        acc = acc_ref[i]
        o = acc[:, :LANES] / acc[:, LANES:]
        r0 = pl.multiple_of(i * FOX_TQ, FOX_TQ)
        o_ref[0, pl.ds(r0, FOX_TQ), :] = jnp.where(first, o[:FOX_TQ], o[FOX_TQ:]).astype(BF16)
        return carry

    lax.fori_loop(0, n_q, finish, 0)


def _fox(proj, kv, qb, kb):
    bsz, seq, _ = proj.shape
    n_pairs = D_TOK // LANES
    n_q = seq // FOX_TQ
    rows = 2 * FOX_TQ
    steps = [(i, i * FOX_TQ, (FOX_TQ, True)) for i in range(n_q)]
    steps += [(i, (i * FOX_TQ) // FOX_TK * FOX_TK, (FOX_TQ, False))
              for i in range(n_q) if (i * FOX_TQ) % FOX_TK]
    steps += [(i, j * FOX_TK, (FOX_TK, False))
              for j in range(seq // FOX_TK) for i in range(n_q) if (i * FOX_TQ) // FOX_TK > j]
    assert sum(kw for _, _, (kw, _) in steps) == sum((i + 1) * FOX_TQ for i in range(n_q))
    itab = jnp.asarray([st[0] for st in steps], jnp.int32)
    ktab = jnp.asarray([st[1] for st in steps], jnp.int32)
    grid_spec = pltpu.PrefetchScalarGridSpec(
        num_scalar_prefetch=2,
        grid=(bsz, n_pairs),
        in_specs=[
            pl.BlockSpec((1, seq, LANES), lambda b, p, *_: (b, 0, p)),
            pl.BlockSpec((1, seq, LANES), lambda b, p, *_: (b, 0, 0)),
            pl.BlockSpec((1, seq, LANES), lambda b, p, *_: (b, 0, p)),
            pl.BlockSpec((1, seq, LANES), lambda b, p, *_: (b, 0, n_pairs + p)),
            pl.BlockSpec((1, seq, LANES), lambda b, p, *_: (b, 0, 0)),
        ],
        out_specs=pl.BlockSpec((1, seq, LANES), lambda b, p, *_: (b, 0, p)),
        scratch_shapes=[
            pltpu.VMEM((n_q, rows, 2 * LANES), BF16),
            pltpu.VMEM((n_q, rows, LANES), F32),
            pltpu.VMEM((n_q, rows, 2 * LANES), F32),
            pltpu.VMEM((2, rows, FOX_TK), F32),
            pltpu.VMEM((2, rows, FOX_TK), BF16),
            pltpu.VMEM((2, rows, LANES), F32),
        ],
    )
    return pl.pallas_call(
        functools.partial(_fox_kernel, kinds=tuple(st[2] for st in steps)),
        grid_spec=grid_spec,
        out_shape=jax.ShapeDtypeStruct((bsz, seq, D_TOK), BF16),
        compiler_params=pltpu.CompilerParams(dimension_semantics=("arbitrary", "arbitrary"),
                                             vmem_limit_bytes=VMEM_LIMIT),
        name="fox",
    )(itab, ktab, proj, qb, kv, kv, kb)


def _mix_ffn_kernel(*refs, final_norm, emit_next):
    (h_ref, tok_ref, qm_ref, mkv_ref, wout_ref, gffn_ref, wup_ref, cw_ref, wdn_ref, gfin_ref), refs = (
        refs[:10], refs[10:])
    if emit_next:
        (gmix_ref, win_ref, gkv_ref, wkvf_ref, bf_ref, tri_ref), refs = refs[:6], refs[6:]
        (o_ref, proj_ref, kv_ref, qb_ref, kb_ref), refs = refs[:5], refs[5:]
        carry_ref, acc_ref, xn_ref, up_ref, act_ref, fcarry_ref = refs
    else:
        o_ref, carry_ref, acc_ref, xn_ref, up_ref, act_ref = refs
    tm = h_ref.shape[1]

    qm = qm_ref[0].astype(F32) * ((D_MEM // N_MEM_HEADS) ** -0.5)
    km = mkv_ref[0, :, :D_MEM]
    vm = mkv_ref[0, :, D_MEM:]
    lane = lax.broadcasted_iota(jnp.int32, (tm, D_MEM), 1)
    mem = jnp.zeros((tm, D_MEM), F32)
    for hh in range(N_MEM_HEADS):
        sel = (lane >= hh * HEAD_DIM) & (lane < (hh + 1) * HEAD_DIM)
        qh = jnp.where(sel, qm, jnp.zeros_like(qm)).astype(BF16)
        s = lax.dot_general(qh, km, (((1,), (1,)), ((), ())), preferred_element_type=F32)
        p = jnp.exp(s - jnp.max(s, axis=1, keepdims=True))
        o = jnp.dot(p.astype(BF16), vm, preferred_element_type=F32)
        mem = jnp.where(sel, o / jnp.sum(p, axis=1, keepdims=True), mem)

    cat = jnp.concatenate([tok_ref[0], mem.astype(BF16)], axis=1)
    hmid = h_ref[0] + jnp.dot(cat, wout_ref[...], preferred_element_type=F32)
    acc_ref[...] = hmid
    xn = _rmsnorm(hmid, gffn_ref[...]).astype(BF16)

    @pl.when(pl.program_id(1) == 0)
    def _():
        carry_ref[...] = jnp.zeros_like(carry_ref)
        if emit_next:
            fcarry_ref[...] = jnp.zeros_like(fcarry_ref)

    xn_ref[...] = xn
    row8 = lax.broadcasted_iota(jnp.int32, (SUBLANES, FF_CHUNK), 0)
    def stage_up(c):
        for half in range(2):
            cols = slice(half * D_FF + c * FF_CHUNK, half * D_FF + (c + 1) * FF_CHUNK)
            up_ref[c % 2, :, half * FF_CHUNK:(half + 1) * FF_CHUNK] = jnp.dot(
                xn_ref[...], wup_ref[:, cols], preferred_element_type=F32)

    stage_up(0)
    for c in range(N_FF_CHUNKS):
        if c + 1 < N_FF_CHUNKS:
            stage_up(c + 1)
        a = up_ref[c % 2, :, :FF_CHUNK]
        g = up_ref[c % 2, :, FF_CHUNK:]
        prev = carry_ref[c]
        carry_ref[c] = g[tm - SUBLANES:, :]
        g1 = pltpu.roll(g, 1, 0)
        g2 = pltpu.roll(g, 2, 0)
        g1 = jnp.concatenate([jnp.where(row8 < 1, pltpu.roll(prev, 1, 0), g1[:SUBLANES]), g1[SUBLANES:]], axis=0)
        g2 = jnp.concatenate([jnp.where(row8 < 2, pltpu.roll(prev, 2, 0), g2[:SUBLANES]), g2[SUBLANES:]], axis=0)
        cw = cw_ref[:, c * FF_CHUNK:(c + 1) * FF_CHUNK]
        gc = g2 * cw[0:1] + g1 * cw[1:2] + g * cw[2:3] + cw[3:4]
        act_ref[:, c * FF_CHUNK:(c + 1) * FF_CHUNK] = (gc * jax.nn.sigmoid(gc) * a).astype(BF16)

    out = acc_ref[...] + jnp.dot(act_ref[...], wdn_ref[...], preferred_element_type=F32)
    if final_norm:
        out = _rmsnorm(out, gfin_ref[...])
    o_ref[0] = out
    if emit_next:
        normed = out * lax.rsqrt(jnp.mean(out * out, axis=-1, keepdims=True) + EPS)
        xn_next = (normed * gmix_ref[...]).astype(BF16)
        proj_ref[0] = jnp.dot(xn_next, win_ref[...], preferred_element_type=F32).astype(BF16)
        hs = (normed * gkv_ref[...]).astype(BF16)
        kvz = jnp.dot(hs, wkvf_ref[...], preferred_element_type=F32)
        kv_ref[0] = kvz[:, :2 * D_TOK].astype(BF16)
        qb_ref[0], kb_ref[0] = _gate_columns(kvz[:, 2 * D_TOK:] + bf_ref[...], tri_ref, fcarry_ref)


def _mix_ffn(h, tok, qmem, memkv, w_out, g_ffn, w_up, conv_w, conv_b, w_down, g_final, final_norm,
             next_layer=None):
    bsz, seq, _ = h.shape
    qmem_block = qmem.shape[-1] // D_MEM - 1
    tm = ROW_TILE
    wup = w_up.astype(BF16)
    wdn = w_down.astype(BF16)
    cw = jnp.concatenate([conv_w, conv_b[None, :], jnp.zeros((SUBLANES - 4, D_FF), conv_w.dtype)],
                         axis=0).astype(F32)
    def row_spec(width):
        return pl.BlockSpec((1, tm, width), lambda b, t: (b, t, 0))

    operands = [h, tok, qmem, memkv, w_out.astype(BF16), g_ffn.reshape(1, D_MODEL), wup, cw, wdn,
                g_final.reshape(1, D_MODEL)]
    in_specs = [
        row_spec(D_MODEL),
        row_spec(D_TOK),
        pl.BlockSpec((1, tm, D_MEM), lambda b, t: (b, t, qmem_block)),
        pl.BlockSpec((1, MEM_TOKENS, 2 * D_MEM), lambda b, t: (b, 0, 0)),
        _const_spec((D_MODEL, D_MODEL)),
        _const_spec((1, D_MODEL)),
        _const_spec((D_MODEL, 2 * D_FF)),
        _const_spec((SUBLANES, D_FF)),
        _const_spec((D_FF, D_MODEL)),
        _const_spec((1, D_MODEL)),
    ]
    out_specs = [row_spec(D_MODEL)]
    out_shape = [jax.ShapeDtypeStruct((bsz, seq, D_MODEL), F32)]
    scratch_shapes = [
        pltpu.VMEM((N_FF_CHUNKS, SUBLANES, FF_CHUNK), F32),
        pltpu.VMEM((tm, D_MODEL), F32),
        pltpu.VMEM((tm, D_MODEL), BF16),
        pltpu.VMEM((2, tm, 2 * FF_CHUNK), F32),
        pltpu.VMEM((tm, D_FF), BF16),
    ]
    if next_layer is not None:
        g_mix_n, w_in_n, g_kv, w_kv, w_fgate, b_fgate = next_layer
        wf, bfr, tri = _gate_operands(w_fgate, b_fgate)
        wkvf = jnp.concatenate([w_kv.astype(BF16), wf], axis=1)
        operands += [g_mix_n.reshape(1, D_MODEL), w_in_n.astype(BF16), g_kv.reshape(1, D_MODEL),
                     wkvf, bfr, tri]
        in_specs += [_const_spec((1, D_MODEL)), _const_spec((D_MODEL, D_MODEL)), _const_spec((1, D_MODEL)),
                     _const_spec((D_MODEL, 2 * D_TOK + LANES)), _const_spec((1, LANES)),
                     _const_spec((CUMSUM_BLOCK, CUMSUM_BLOCK))]
        out_specs += [row_spec(D_MODEL), row_spec(2 * D_TOK), row_spec(LANES), row_spec(LANES)]
        out_shape += [jax.ShapeDtypeStruct((bsz, seq, n), BF16) for n in (D_MODEL, 2 * D_TOK, LANES, LANES)]
        scratch_shapes.append(pltpu.VMEM((1, LANES), F32))
    outs = pl.pallas_call(
        functools.partial(_mix_ffn_kernel, final_norm=final_norm, emit_next=next_layer is not None),
        grid=(bsz, seq // tm),
        in_specs=in_specs,
        out_specs=out_specs,
        out_shape=out_shape,
        scratch_shapes=scratch_shapes,
        compiler_params=pltpu.CompilerParams(dimension_semantics=("arbitrary", "arbitrary"),
                                             vmem_limit_bytes=VMEM_LIMIT),
        name="mix_ffn",
    )(*operands)
    return outs if next_layer is not None else outs[0]


def kernel(x, mem, g_mix, w_in, w_out, g_mem, w_mem_kv, s5_a_re, s5_a_im, s5_log_dt, s5_b_re, s5_b_im,
           s5_c_re, s5_c_im, s5_d, w_glu, g_kv, w_kv, w_fgate, b_fgate, g_ffn, w_ffn_up, conv_w, conv_b,
           w_ffn_down, g_final):
    depth = w_in.shape[0]
    assert depth == 2, "one S5 layer followed by one attention layer"
    memkv = _memkv(mem, g_mem, w_mem_kv)

    def ffn_args(l):
        return (memkv[l], w_out[l], g_ffn[l], w_ffn_up[l], conv_w[l], conv_b[l], w_ffn_down[l], g_final)

    tok, qmem = _s5_glu(x, g_mix[0], w_in[0], s5_a_re[0], s5_a_im[0], s5_log_dt[0], s5_b_re[0],
                        s5_b_im[0], s5_c_re[0], s5_c_im[0], s5_d[0], w_glu[0])
    q_scale = jnp.where(jnp.arange(D_MODEL) < D_TOK, (HEAD_DIM ** -0.5) * LOG2E, 1.0).astype(F32)
    h, proj, kv, qb, kb = _mix_ffn(x, tok, qmem, *ffn_args(0), final_norm=False,
                                   next_layer=(g_mix[1], w_in[1] * q_scale[None, :], g_kv, w_kv,
                                               w_fgate, b_fgate))
    tok = _fox(proj, kv, qb, kb)
    return _mix_ffn(h, tok, proj, *ffn_args(1), final_norm=True)
```

```python
import functools
import math

import jax
import jax.numpy as jnp
from jax import lax
from jax.experimental import pallas as pl
from jax.experimental.pallas import tpu as pltpu

F32 = jnp.float32
BF16 = jnp.bfloat16

D_MODEL = 1024
HEAD_DIM = 64
D_MEM = 256
N_MEM_HEADS = 4
MEM_TOKENS = 256
D_TOK = D_MODEL - D_MEM
S5_GROUP = 16
S5_GROUPS = D_TOK // S5_GROUP
S5_STATE = 64
N_FOX_HEADS = D_TOK // HEAD_DIM
D_FF = 2816
EPS = 1e-6
LOG2E = math.log2(math.e)

LANES = 128
SUBLANES = 8
MXU_DIM = 256

ROW_TILE = 512
S5_STEPS = MXU_DIM // SUBLANES
S5_SLABS = D_TOK // MXU_DIM
S5_SLAB_STATE = (MXU_DIM // S5_GROUP) * S5_STATE
S5_STATE_ALL = S5_GROUPS * S5_STATE
S5_STAGES = 2
SCAN_LANES = 512
FF_CHUNK = 256
N_FF_CHUNKS = D_FF // FF_CHUNK
FOX_TQ = 256
FOX_TK = 512
FOX_UNROLL = 16
FOX_UNROLL_DIAG = 4
GATE_COLS = 8
CUMSUM_BLOCK = MXU_DIM
VMEM_LIMIT = 56 * 1024 * 1024


def _rmsnorm(x, g):
    ms = jnp.mean(x * x, axis=-1, keepdims=True)
    return x * lax.rsqrt(ms + EPS) * g


def _const_spec(shape):
    zeros = (0,) * len(shape)
    return pl.BlockSpec(shape, lambda *_: zeros, pipeline_mode=pl.Buffered(1))


def _memkv_kernel(mem_ref, g_ref, w_ref, o_ref):
    mn = _rmsnorm(mem_ref[0], g_ref[...]).astype(BF16)
    for l in range(w_ref.shape[0]):
        o_ref[l, 0] = jnp.dot(mn, w_ref[l], preferred_element_type=F32).astype(BF16)


def _memkv(mem, g_mem, w_mem_kv):
    bsz = mem.shape[0]
    depth = w_mem_kv.shape[0]
    return pl.pallas_call(
        _memkv_kernel,
        grid=(bsz,),
        in_specs=[
            pl.BlockSpec((1, MEM_TOKENS, D_MODEL), lambda b: (b, 0, 0)),
            _const_spec((1, D_MODEL)),
            _const_spec((depth, D_MODEL, 2 * D_MEM)),
        ],
        out_specs=pl.BlockSpec((depth, 1, MEM_TOKENS, 2 * D_MEM), lambda b: (0, b, 0, 0)),
        out_shape=jax.ShapeDtypeStruct((depth, bsz, MEM_TOKENS, 2 * D_MEM), BF16),
        compiler_params=pltpu.CompilerParams(dimension_semantics=("arbitrary",)),
        name="memkv",
    )(mem, g_mem.reshape(1, D_MODEL), w_mem_kv.astype(BF16))


def _s5_discretise(a_re, a_im, log_dt, b_re, b_im, c_re, c_im):
    dt = jnp.exp(log_dt.astype(F32))[:, None]
    lam_re = jnp.minimum(a_re.astype(F32), -1e-4)
    lam_im = a_im.astype(F32)
    mag = jnp.exp(lam_re * dt)
    ph = lam_im * dt
    ab_re, ab_im = mag * jnp.cos(ph), mag * jnp.sin(ph)
    den = lam_re * lam_re + lam_im * lam_im
    z_re = ((ab_re - 1.0) * lam_re + ab_im * lam_im) / den
    z_im = (ab_im * lam_re - (ab_re - 1.0) * lam_im) / den
    br, bi = b_re.astype(F32), b_im.astype(F32)
    bb_re = z_re[..., None] * br - z_im[..., None] * bi
    bb_im = z_re[..., None] * bi + z_im[..., None] * br
    gps = MXU_DIM // S5_GROUP
    eye = jnp.eye(gps, dtype=F32)

    def pack_b(bb):
        bb = bb.reshape(S5_SLABS, gps, S5_STATE, S5_GROUP)
        return jnp.einsum('sgpi,gh->sgihp', bb, eye).reshape(S5_SLABS, MXU_DIM, S5_SLAB_STATE)

    def pack_c(cc):
        cc = cc.reshape(S5_SLABS, gps, S5_GROUP, S5_STATE)
        return jnp.einsum('sgip,gh->sgphi', cc, eye).reshape(S5_SLABS, S5_SLAB_STATE, MXU_DIM)

    b_bd = jnp.concatenate([pack_b(bb_re), pack_b(bb_im)], axis=2).astype(BF16)
    c_bd = jnp.concatenate([pack_c(c_re.astype(F32)), -pack_c(c_im.astype(F32))], axis=1).astype(BF16)
    return ab_re.reshape(1, S5_STATE_ALL), ab_im.reshape(1, S5_STATE_ALL), b_bd, c_bd


def _s5_glu_kernel(x_ref, gmix_ref, win_ref, perm_ref, permt_ref, bbd_ref, cbd_ref, are_ref, aim_ref,
                   d_ref, wglu_ref, o_ref, qm_ref, hre_ref, him_ref, *bufs):
    rows = SUBLANES * S5_STEPS
    sre_refs, sim_refs, up_refs = bufs[0:S5_STAGES], bufs[S5_STAGES:2 * S5_STAGES], bufs[2 * S5_STAGES:]
    i = pl.program_id(0)

    @pl.when(i == 0)
    def _():
        hre_ref[...] = jnp.zeros_like(hre_ref)
        him_ref[...] = jnp.zeros_like(him_ref)
        for ref in bufs:
            ref[...] = jnp.zeros_like(ref)

    def permute_in(up_ref):
        xn = _rmsnorm(x_ref[...].reshape(rows, D_MODEL), gmix_ref[...]).astype(BF16)
        proj = jnp.dot(xn, win_ref[...], preferred_element_type=F32).astype(BF16)
        qm_ref[...] = proj[:, D_TOK:].reshape(SUBLANES, S5_STEPS, D_MEM)
        up_ref[...] = jnp.dot(perm_ref[...], proj[:, :D_TOK], preferred_element_type=F32).astype(BF16)

    def project_in(s, sre_ref, sim_ref, up_ref):
        bu = jnp.dot(up_ref[:, s * MXU_DIM:(s + 1) * MXU_DIM], bbd_ref[s], preferred_element_type=F32)
        sre_ref[:, s * S5_SLAB_STATE:(s + 1) * S5_SLAB_STATE] = bu[:, :S5_SLAB_STATE]
        sim_ref[:, s * S5_SLAB_STATE:(s + 1) * S5_SLAB_STATE] = bu[:, S5_SLAB_STATE:]

    def scan(c, sre_ref, sim_ref):
        cs = slice(c * SCAN_LANES, (c + 1) * SCAN_LANES)
        ar = jnp.broadcast_to(are_ref[:, cs], (SUBLANES, SCAN_LANES))
        ai = jnp.broadcast_to(aim_ref[:, cs], (SUBLANES, SCAN_LANES))
        hr = hre_ref[:, cs]
        hi = him_ref[:, cs]
        for t in range(S5_STEPS):
            rs = slice(t * SUBLANES, (t + 1) * SUBLANES)
            br = sre_ref[rs, cs]
            bi = sim_ref[rs, cs]
            hr, hi = ar * hr - ai * hi + br, ar * hi + ai * hr + bi
            sre_ref[rs, cs] = hr
            sim_ref[rs, cs] = hi
        hre_ref[:, cs] = hr
        him_ref[:, cs] = hi

    def project_out(s, sre_ref, sim_ref):
        ss = slice(s * S5_SLAB_STATE, (s + 1) * S5_SLAB_STATE)
        y_re = jnp.dot(sre_ref[:, ss].astype(BF16), cbd_ref[s, :S5_SLAB_STATE], preferred_element_type=F32)
        y_im = jnp.dot(sim_ref[:, ss].astype(BF16), cbd_ref[s, S5_SLAB_STATE:], preferred_element_type=F32)
        return y_re + y_im

    def gate_out(ys, up_ref):
        y = jnp.concatenate(ys, axis=1) + d_ref[...] * up_ref[...].astype(F32)
        toks = []
        for half in range(2):
            g = jax.nn.gelu(y[half * rows // 2:(half + 1) * rows // 2])
            z = jnp.dot(g.astype(BF16), wglu_ref[...], preferred_element_type=F32)
            toks.append((g * jax.nn.sigmoid(z)).astype(BF16))
        tok = jnp.concatenate(toks, axis=0)
        out = jnp.dot(permt_ref[...], tok, preferred_element_type=F32).astype(BF16)
        o_ref[...] = out.reshape(SUBLANES, S5_STEPS, D_TOK)

    scans_per_slab = S5_STATE_ALL // SCAN_LANES // S5_SLABS
    for r in range(S5_STAGES):
        @pl.when(i % S5_STAGES == r)
        def _(r=r):
            a, b = r, (r - 1) % S5_STAGES
            permute_in(up_refs[a])
            ys = []
            for s in range(S5_SLABS):
                project_in(s, sre_refs[a], sim_refs[a], up_refs[a])
                for k in range(scans_per_slab):
                    scan(s * scans_per_slab + k, sre_refs[b], sim_refs[b])
                ys.append(project_out(s, sre_refs[b], sim_refs[b]))
            gate_out(ys, up_refs[b])


def _s5_glu(h, g_mix, w_in, a_re, a_im, log_dt, b_re, b_im, c_re, c_im, d_skip, w_glu):
    bsz, seq, _ = h.shape
    assert bsz == SUBLANES, "the scan keeps one batch element per sublane"
    rows = SUBLANES * S5_STEPS
    n_blocks = seq // S5_STEPS
    lag = S5_STAGES - 1
    ab_re, ab_im, b_bd, c_bd = _s5_discretise(a_re, a_im, log_dt, b_re, b_im, c_re, c_im)
    src = (jnp.arange(rows) % SUBLANES) * S5_STEPS + jnp.arange(rows) // SUBLANES
    perm = (src[:, None] == jnp.arange(rows)[None, :]).astype(BF16)
    stage_bufs = ([pltpu.VMEM((rows, S5_STATE_ALL), F32)] * (2 * S5_STAGES)
                  + [pltpu.VMEM((rows, D_TOK), BF16)] * S5_STAGES)
    return pl.pallas_call(
        _s5_glu_kernel,
        grid=(n_blocks + lag,),
        in_specs=[
            pl.BlockSpec((bsz, S5_STEPS, D_MODEL), lambda i: (0, jnp.minimum(i, n_blocks - 1), 0)),
            _const_spec((1, D_MODEL)),
            _const_spec((D_MODEL, D_MODEL)),
            _const_spec((rows, rows)),
            _const_spec((rows, rows)),
            _const_spec((S5_SLABS, MXU_DIM, 2 * S5_SLAB_STATE)),
            _const_spec((S5_SLABS, 2 * S5_SLAB_STATE, MXU_DIM)),
            _const_spec((1, S5_STATE_ALL)),
            _const_spec((1, S5_STATE_ALL)),
            _const_spec((1, D_TOK)),
            _const_spec((D_TOK, D_TOK)),
        ],
        out_specs=[
            pl.BlockSpec((bsz, S5_STEPS, D_TOK), lambda i: (0, jnp.maximum(i - lag, 0), 0)),
            pl.BlockSpec((bsz, S5_STEPS, D_MEM), lambda i: (0, jnp.minimum(i, n_blocks - 1), 0)),
        ],
        out_shape=[
            jax.ShapeDtypeStruct((bsz, seq, D_TOK), BF16),
            jax.ShapeDtypeStruct((bsz, seq, D_MEM), BF16),
        ],
        scratch_shapes=[
            pltpu.VMEM((SUBLANES, S5_STATE_ALL), F32),
            pltpu.VMEM((SUBLANES, S5_STATE_ALL), F32),
        ] + stage_bufs,
        compiler_params=pltpu.CompilerParams(dimension_semantics=("arbitrary",),
                                             vmem_limit_bytes=VMEM_LIMIT),
        name="s5_glu",
    )(h, g_mix.reshape(1, D_MODEL), w_in.astype(BF16), perm, perm.T, b_bd, c_bd, ab_re, ab_im,
      d_skip.reshape(1, D_TOK).astype(F32), w_glu.astype(BF16))


def _split3(x):
    hi = x.astype(BF16)
    r = x - hi.astype(F32)
    mid = r.astype(BF16)
    lo = (r - mid.astype(F32)).astype(BF16)
    return hi, mid, lo


def _gate_columns(z, tri_ref, carry_ref):
    n_rows = z.shape[0]
    logf = -(jnp.maximum(-z, 0.0) + jnp.log1p(jnp.exp(-jnp.abs(z))))
    tri = tri_ref[...]
    parts = jnp.concatenate(_split3(logf), axis=1)
    running = carry_ref[...]
    blocks = []
    for r in range(n_rows // CUMSUM_BLOCK):
        c3 = jnp.dot(tri, parts[r * CUMSUM_BLOCK:(r + 1) * CUMSUM_BLOCK], preferred_element_type=F32)
        csum = c3[:, :LANES] + c3[:, LANES:2 * LANES] + c3[:, 2 * LANES:]
        blocks.append(csum + running)
        running = blocks[-1][CUMSUM_BLOCK - 1:CUMSUM_BLOCK, :]
    carry_ref[...] = running
    fcum = jnp.concatenate(blocks, axis=0)
    hi, mid, lo = (part.astype(F32) for part in _split3(fcum * LOG2E))
    lane = lax.broadcasted_iota(jnp.int32, fcum.shape, 1)
    j = lane & (GATE_COLS - 1)
    valid = lane < N_FOX_HEADS * GATE_COLS
    one = jnp.ones_like(hi)
    zero = jnp.zeros_like(hi)
    qb = jnp.where(j == 0, hi, jnp.where(j == 1, mid, jnp.where(j == 2, lo, jnp.where(j < 6, one, zero))))
    kb = jnp.where(j < 3, one, jnp.where(j == 3, -hi, jnp.where(j == 4, -mid, jnp.where(j == 5, -lo, zero))))
    return jnp.where(valid, qb, zero).astype(BF16), jnp.where(valid, kb, zero).astype(BF16)


def _gate_operands(w_fgate, b_fgate):
    pad = LANES - N_FOX_HEADS * GATE_COLS
    wf = jnp.pad(jnp.repeat(w_fgate, GATE_COLS, axis=1), ((0, 0), (0, pad))).astype(BF16)
    bfr = jnp.pad(jnp.repeat(b_fgate, GATE_COLS), (0, pad)).reshape(1, LANES).astype(F32)
    tri = (jnp.arange(CUMSUM_BLOCK)[:, None] >= jnp.arange(CUMSUM_BLOCK)[None, :]).astype(BF16)
    return wf, bfr, tri


def _fox_kernel(itab_ref, ktab_ref, q_ref, qb_ref, k_ref, v_ref, kb_ref, o_ref,
                qq_ref, m_ref, acc_ref, s_ref, p_ref, a_ref, *, kinds):
    hp = pl.program_id(1)
    seq = q_ref.shape[1]
    n_q = seq // FOX_TQ
    rows = 2 * FOX_TQ
    n_steps = len(kinds)

    def scores(t, kw):
        c0 = pl.multiple_of(ktab_ref[t], FOX_TQ)
        kk = jnp.concatenate([k_ref[0, pl.ds(c0, kw), :], kb_ref[0, pl.ds(c0, kw), :]], axis=1)
        return lax.dot_general(qq_ref[itab_ref[t]], kk, (((1,), (1,)), ((), ())), preferred_element_type=F32)

    def lane_blocks(s):
        return [s[:, n * LANES:(n + 1) * LANES] for n in range(s.shape[1] // LANES)]

    def row_max(blocks):
        mx = functools.reduce(jnp.maximum, blocks)
        return jnp.broadcast_to(jnp.max(mx, axis=1, keepdims=True), (rows, LANES))

    def probs(blocks, m):
        return jnp.concatenate([jnp.exp2(blk - m).astype(BF16) for blk in blocks], axis=1)

    def pv_dot(p, t, kw):
        c0 = pl.multiple_of(ktab_ref[t], FOX_TQ)
        vv = jnp.concatenate([v_ref[0, pl.ds(c0, kw), :], jnp.ones((kw, LANES), BF16)], axis=1)
        return jnp.dot(p, vv, preferred_element_type=F32)

    lane = lax.broadcasted_iota(jnp.int32, (FOX_TQ, LANES), 1)
    first = lane < HEAD_DIM
    gate_head = lane // GATE_COLS

    def build_q(i, carry):
        r0 = pl.multiple_of(i * FOX_TQ, FOX_TQ)
        q2 = q_ref[0, pl.ds(r0, FOX_TQ), :].astype(F32)
        qb = qb_ref[0, pl.ds(r0, FOX_TQ), :].astype(F32)
        zero = jnp.zeros_like(q2)
        qa = jnp.concatenate([jnp.where(first, q2, zero), jnp.where(gate_head == 2 * hp, qb, zero)], axis=1)
        qo = jnp.concatenate([jnp.where(first, zero, q2), jnp.where(gate_head == 2 * hp + 1, qb, zero)], axis=1)
        qq_ref[i] = jnp.concatenate([qa, qo], axis=0).astype(BF16)
        m_ref[i] = jnp.full((rows, LANES), -jnp.inf, F32)
        acc_ref[i] = jnp.zeros((rows, 2 * LANES), F32)
        return carry

    lax.fori_loop(0, n_q, build_q, 0)

    visible = ((lax.broadcasted_iota(jnp.int32, (rows, FOX_TQ), 0) & (FOX_TQ - 1))
               >= lax.broadcasted_iota(jnp.int32, (rows, FOX_TQ), 1))

    def stage_scores(t, slot, kind):
        kw, _ = kind
        s_ref[slot, :, :kw] = scores(t, kw)

    def load_blocks(slot, kind):
        kw, masked = kind
        s = s_ref[slot, :, :kw]
        if masked:
            assert kw == FOX_TQ
            s = jnp.where(visible, s, -jnp.inf)
        return lane_blocks(s)

    def softmax_update(t, slot, kind, blocks):
        kw, _ = kind
        i = itab_ref[t]
        m_old = m_ref[i]
        m_new = jnp.maximum(m_old, row_max(blocks))
        m_ref[i] = m_new
        a_ref[slot] = jnp.exp2(m_old - m_new)
        p_ref[slot, :, :kw] = probs(blocks, m_new)

    def stage_pv(t, slot, kind):
        kw, _ = kind
        i = itab_ref[t]
        alpha = a_ref[slot]
        acc_ref[i] = (jnp.concatenate([alpha, alpha], axis=1) * acc_ref[i]
                      + pv_dot(p_ref[slot, :, :kw], t, kw))

    def tick(t, parity, kind2, kind1, kind0):
        assert parity in (0, 1)
        stage_scores(t + 2, parity, kind2)
        blocks = load_blocks(1 - parity, kind1)
        stage_pv(t, parity, kind0)
        softmax_update(t + 1, 1 - parity, kind1, blocks)

    stage_scores(0, 0, kinds[0])
    stage_scores(1, 1, kinds[1])
    softmax_update(0, 0, kinds[0], load_blocks(0, kinds[0]))
    t = 0
    while t < n_steps - 2:
        trio = (kinds[t + 2], kinds[t + 1], kinds[t])
        run = 1
        while t + run < n_steps - 2 and (kinds[t + run + 2], kinds[t + run + 1], kinds[t + run]) == trio:
            run += 1
        unroll = FOX_UNROLL if trio[0][0] == FOX_TK else FOX_UNROLL_DIAG
        looped = (run // unroll) * unroll if t % 2 == 0 else 0
        if looped:
            def body(u, carry, t=t, unroll=unroll, trio=trio):
                for k in range(unroll):
                    tick(t + unroll * u + k, k % 2, *trio)
                return carry

            lax.fori_loop(0, looped // unroll, body, 0)
        for k in range(looped, run):
            tick(t + k, (t + k) % 2, *trio)
        t += run
    last = n_steps - 1
    softmax_update(last, last % 2, kinds[last], load_blocks(last % 2, kinds[last]))
    stage_pv(last - 1, (last - 1) % 2, kinds[last - 1])
    stage_pv(last, last % 2, kinds[last])

    def finish(i, carry):
        acc = acc_ref[i]
        o = acc[:, :LANES] / acc[:, LANES:]
        r0 = pl.multiple_of(i * FOX_TQ, FOX_TQ)
        o_ref[0, pl.ds(r0, FOX_TQ), :] = jnp.where(first, o[:FOX_TQ], o[FOX_TQ:]).astype(BF16)
        return carry

    lax.fori_loop(0, n_q, finish, 0)


def _fox(proj, kv, qb, kb):
    bsz, seq, _ = proj.shape
    n_pairs = D_TOK // LANES
    n_q = seq // FOX_TQ
    rows = 2 * FOX_TQ
    steps = [(i, i * FOX_TQ, (FOX_TQ, True)) for i in range(n_q)]
    steps += [(i, (i * FOX_TQ) // FOX_TK * FOX_TK, (FOX_TQ, False))
              for i in range(n_q) if (i * FOX_TQ) % FOX_TK]
    steps += [(i, j * FOX_TK, (FOX_TK, False))
              for j in range(seq // FOX_TK) for i in range(n_q) if (i * FOX_TQ) // FOX_TK > j]
    assert sum(kw for _, _, (kw, _) in steps) == sum((i + 1) * FOX_TQ for i in range(n_q))
    itab = jnp.asarray([st[0] for st in steps], jnp.int32)
    ktab = jnp.asarray([st[1] for st in steps], jnp.int32)
    grid_spec = pltpu.PrefetchScalarGridSpec(
        num_scalar_prefetch=2,
        grid=(bsz, n_pairs),
        in_specs=[
            pl.BlockSpec((1, seq, LANES), lambda b, p, *_: (b, 0, p)),
            pl.BlockSpec((1, seq, LANES), lambda b, p, *_: (b, 0, 0)),
            pl.BlockSpec((1, seq, LANES), lambda b, p, *_: (b, 0, p)),
            pl.BlockSpec((1, seq, LANES), lambda b, p, *_: (b, 0, n_pairs + p)),
            pl.BlockSpec((1, seq, LANES), lambda b, p, *_: (b, 0, 0)),
        ],
        out_specs=pl.BlockSpec((1, seq, LANES), lambda b, p, *_: (b, 0, p)),
        scratch_shapes=[
            pltpu.VMEM((n_q, rows, 2 * LANES), BF16),
            pltpu.VMEM((n_q, rows, LANES), F32),
            pltpu.VMEM((n_q, rows, 2 * LANES), F32),
            pltpu.VMEM((2, rows, FOX_TK), F32),
            pltpu.VMEM((2, rows, FOX_TK), BF16),
            pltpu.VMEM((2, rows, LANES), F32),
        ],
    )
    return pl.pallas_call(
        functools.partial(_fox_kernel, kinds=tuple(st[2] for st in steps)),
        grid_spec=grid_spec,
        out_shape=jax.ShapeDtypeStruct((bsz, seq, D_TOK), BF16),
        compiler_params=pltpu.CompilerParams(dimension_semantics=("arbitrary", "arbitrary"),
                                             vmem_limit_bytes=VMEM_LIMIT),
        name="fox",
    )(itab, ktab, proj, qb, kv, kv, kb)


def _mix_ffn_kernel(*refs, final_norm, emit_next):
    (h_ref, tok_ref, qm_ref, mkv_ref, wout_ref, gffn_ref, wup_ref, cw_ref, wdn_ref, gfin_ref), refs = (
        refs[:10], refs[10:])
    if emit_next:
        (gmix_ref, win_ref, gkv_ref, wkvf_ref, bf_ref, tri_ref), refs = refs[:6], refs[6:]
        (o_ref, proj_ref, kv_ref, qb_ref, kb_ref), refs = refs[:5], refs[5:]
        carry_ref, acc_ref, xn_ref, up_ref, act_ref, fcarry_ref = refs
    else:
        o_ref, carry_ref, acc_ref, xn_ref, up_ref, act_ref = refs
    tm = h_ref.shape[1]

    qm = qm_ref[0].astype(F32) * ((D_MEM // N_MEM_HEADS) ** -0.5)
    km = mkv_ref[0, :, :D_MEM]
    vm = mkv_ref[0, :, D_MEM:]
    lane = lax.broadcasted_iota(jnp.int32, (tm, D_MEM), 1)
    head = [(lane >= hh * HEAD_DIM) & (lane < (hh + 1) * HEAD_DIM) for hh in range(N_MEM_HEADS)]
    qs = jnp.concatenate([jnp.where(sel, qm, jnp.zeros_like(qm)) for sel in head], axis=0).astype(BF16)
    s = lax.dot_general(qs, km, (((1,), (1,)), ((), ())), preferred_element_type=F32)
    p = jnp.exp(s - jnp.max(s, axis=1, keepdims=True))
    o = jnp.dot(p.astype(BF16), vm, preferred_element_type=F32) / jnp.sum(p, axis=1, keepdims=True)
    mem = jnp.zeros((tm, D_MEM), F32)
    for hh, sel in enumerate(head):
        mem = jnp.where(sel, o[hh * tm:(hh + 1) * tm], mem)

    cat = jnp.concatenate([tok_ref[0], mem.astype(BF16)], axis=1)

    @pl.when(pl.program_id(1) == 0)
    def _():
        carry_ref[...] = jnp.zeros_like(carry_ref)
        if emit_next:
            fcarry_ref[...] = jnp.zeros_like(fcarry_ref)

    def stage_up(c, row_blocks=1):
        for r in range(row_blocks):
            rs = slice(r * tm // row_blocks, (r + 1) * tm // row_blocks)
            for half in range(2):
                cols = slice(half * D_FF + c * FF_CHUNK, half * D_FF + (c + 1) * FF_CHUNK)
                up_ref[c % 2, rs, half * FF_CHUNK:(half + 1) * FF_CHUNK] = jnp.dot(
                    xn_ref[rs], wup_ref[:, cols], preferred_element_type=F32)

    head_blocks = 2
    for r in range(head_blocks):
        rs = slice(r * tm // head_blocks, (r + 1) * tm // head_blocks)
        hmid = h_ref[0, rs] + jnp.dot(cat[rs], wout_ref[...], preferred_element_type=F32)
        acc_ref[rs] = hmid
        xn_ref[rs] = _rmsnorm(hmid, gffn_ref[...]).astype(BF16)

    row8 = lax.broadcasted_iota(jnp.int32, (SUBLANES, FF_CHUNK), 0)
    stage_up(0, head_blocks)
    for c in range(N_FF_CHUNKS):
        if c + 1 < N_FF_CHUNKS:
            stage_up(c + 1)
        a = up_ref[c % 2, :, :FF_CHUNK]
        g = up_ref[c % 2, :, FF_CHUNK:]
        prev = carry_ref[c]
        carry_ref[c] = g[tm - SUBLANES:, :]
        g1 = pltpu.roll(g, 1, 0)
        g2 = pltpu.roll(g, 2, 0)
        g1 = jnp.concatenate([jnp.where(row8 < 1, pltpu.roll(prev, 1, 0), g1[:SUBLANES]), g1[SUBLANES:]], axis=0)
        g2 = jnp.concatenate([jnp.where(row8 < 2, pltpu.roll(prev, 2, 0), g2[:SUBLANES]), g2[SUBLANES:]], axis=0)
        cw = cw_ref[:, c * FF_CHUNK:(c + 1) * FF_CHUNK]
        gc = g2 * cw[0:1] + g1 * cw[1:2] + g * cw[2:3] + cw[3:4]
        act_ref[:, c * FF_CHUNK:(c + 1) * FF_CHUNK] = (gc * jax.nn.sigmoid(gc) * a).astype(BF16)

    out = acc_ref[...] + jnp.dot(act_ref[...], wdn_ref[...], preferred_element_type=F32)
    if final_norm:
        out = _rmsnorm(out, gfin_ref[...])
    o_ref[0] = out
    if emit_next:
        normed = out * lax.rsqrt(jnp.mean(out * out, axis=-1, keepdims=True) + EPS)
        xn_next = (normed * gmix_ref[...]).astype(BF16)
        proj_ref[0] = jnp.dot(xn_next, win_ref[...], preferred_element_type=F32).astype(BF16)
        hs = (normed * gkv_ref[...]).astype(BF16)
        kvz = jnp.dot(hs, wkvf_ref[...], preferred_element_type=F32)
        kv_ref[0] = kvz[:, :2 * D_TOK].astype(BF16)
        qb_ref[0], kb_ref[0] = _gate_columns(kvz[:, 2 * D_TOK:] + bf_ref[...], tri_ref, fcarry_ref)


def _mix_ffn(h, tok, qmem, memkv, w_out, g_ffn, w_up, conv_w, conv_b, w_down, g_final, final_norm,
             next_layer=None):
    bsz, seq, _ = h.shape
    qmem_block = qmem.shape[-1] // D_MEM - 1
    tm = ROW_TILE
    wup = w_up.astype(BF16)
    wdn = w_down.astype(BF16)
    cw = jnp.concatenate([conv_w, conv_b[None, :], jnp.zeros((SUBLANES - 4, D_FF), conv_w.dtype)],
                         axis=0).astype(F32)
    def row_spec(width):
        return pl.BlockSpec((1, tm, width), lambda b, t: (b, t, 0))

    operands = [h, tok, qmem, memkv, w_out.astype(BF16), g_ffn.reshape(1, D_MODEL), wup, cw, wdn,
                g_final.reshape(1, D_MODEL)]
    in_specs = [
        row_spec(D_MODEL),
        row_spec(D_TOK),
        pl.BlockSpec((1, tm, D_MEM), lambda b, t: (b, t, qmem_block)),
        pl.BlockSpec((1, MEM_TOKENS, 2 * D_MEM), lambda b, t: (b, 0, 0)),
        _const_spec((D_MODEL, D_MODEL)),
        _const_spec((1, D_MODEL)),
        _const_spec((D_MODEL, 2 * D_FF)),
        _const_spec((SUBLANES, D_FF)),
        _const_spec((D_FF, D_MODEL)),
        _const_spec((1, D_MODEL)),
    ]
    out_specs = [row_spec(D_MODEL)]
    out_shape = [jax.ShapeDtypeStruct((bsz, seq, D_MODEL), F32)]
    scratch_shapes = [
        pltpu.VMEM((N_FF_CHUNKS, SUBLANES, FF_CHUNK), F32),
        pltpu.VMEM((tm, D_MODEL), F32),
        pltpu.VMEM((tm, D_MODEL), BF16),
        pltpu.VMEM((2, tm, 2 * FF_CHUNK), F32),
        pltpu.VMEM((tm, D_FF), BF16),
    ]
    if next_layer is not None:
        g_mix_n, w_in_n, g_kv, w_kv, w_fgate, b_fgate = next_layer
        wf, bfr, tri = _gate_operands(w_fgate, b_fgate)
        wkvf = jnp.concatenate([w_kv.astype(BF16), wf], axis=1)
        operands += [g_mix_n.reshape(1, D_MODEL), w_in_n.astype(BF16), g_kv.reshape(1, D_MODEL),
                     wkvf, bfr, tri]
        in_specs += [_const_spec((1, D_MODEL)), _const_spec((D_MODEL, D_MODEL)), _const_spec((1, D_MODEL)),
                     _const_spec((D_MODEL, 2 * D_TOK + LANES)), _const_spec((1, LANES)),
                     _const_spec((CUMSUM_BLOCK, CUMSUM_BLOCK))]
        out_specs += [row_spec(D_MODEL), row_spec(2 * D_TOK), row_spec(LANES), row_spec(LANES)]
        out_shape += [jax.ShapeDtypeStruct((bsz, seq, n), BF16) for n in (D_MODEL, 2 * D_TOK, LANES, LANES)]
        scratch_shapes.append(pltpu.VMEM((1, LANES), F32))
    outs = pl.pallas_call(
        functools.partial(_mix_ffn_kernel, final_norm=final_norm, emit_next=next_layer is not None),
        grid=(bsz, seq // tm),
        in_specs=in_specs,
        out_specs=out_specs,
        out_shape=out_shape,
        scratch_shapes=scratch_shapes,
        compiler_params=pltpu.CompilerParams(dimension_semantics=("arbitrary", "arbitrary"),
                                             vmem_limit_bytes=VMEM_LIMIT),
        name="mix_ffn",
    )(*operands)
    return outs if next_layer is not None else outs[0]


def kernel(x, mem, g_mix, w_in, w_out, g_mem, w_mem_kv, s5_a_re, s5_a_im, s5_log_dt, s5_b_re, s5_b_im,
           s5_c_re, s5_c_im, s5_d, w_glu, g_kv, w_kv, w_fgate, b_fgate, g_ffn, w_ffn_up, conv_w, conv_b,
           w_ffn_down, g_final):
    depth = w_in.shape[0]
    assert depth == 2, "one S5 layer followed by one attention layer"
    memkv = _memkv(mem, g_mem, w_mem_kv)

    def ffn_args(l):
        return (memkv[l], w_out[l], g_ffn[l], w_ffn_up[l], conv_w[l], conv_b[l], w_ffn_down[l], g_final)

    tok, qmem = _s5_glu(x, g_mix[0], w_in[0], s5_a_re[0], s5_a_im[0], s5_log_dt[0], s5_b_re[0],
                        s5_b_im[0], s5_c_re[0], s5_c_im[0], s5_d[0], w_glu[0])
    q_scale = jnp.where(jnp.arange(D_MODEL) < D_TOK, (HEAD_DIM ** -0.5) * LOG2E, 1.0).astype(F32)
    h, proj, kv, qb, kb = _mix_ffn(x, tok, qmem, *ffn_args(0), final_norm=False,
                                   next_layer=(g_mix[1], w_in[1] * q_scale[None, :], g_kv, w_kv,
                                               w_fgate, b_fgate))
    tok = _fox(proj, kv, qb, kb)
    return _mix_ffn(h, tok, proj, *ffn_args(1), final_norm=True)
```

```python
import functools
import math

import jax
import jax.numpy as jnp
from jax import lax
from jax.experimental import pallas as pl
from jax.experimental.pallas import tpu as pltpu

F32 = jnp.float32
BF16 = jnp.bfloat16

D_MODEL = 1024
HEAD_DIM = 64
D_MEM = 256
N_MEM_HEADS = 4
MEM_TOKENS = 256
D_TOK = D_MODEL - D_MEM
S5_GROUP = 16
S5_GROUPS = D_TOK // S5_GROUP
S5_STATE = 64
N_FOX_HEADS = D_TOK // HEAD_DIM
D_FF = 2816
EPS = 1e-6
LOG2E = math.log2(math.e)

LANES = 128
SUBLANES = 8
MXU_DIM = 256

ROW_TILE = 512
S5_STEPS = MXU_DIM // SUBLANES
S5_SLABS = D_TOK // MXU_DIM
S5_SLAB_STATE = (MXU_DIM // S5_GROUP) * S5_STATE
S5_STATE_ALL = S5_GROUPS * S5_STATE
S5_STAGES = 2
SCAN_LANES = 512
FF_CHUNK = 256
N_FF_CHUNKS = D_FF // FF_CHUNK
FOX_TQ = 256
FOX_TK = 512
FOX_UNROLL = 16
FOX_UNROLL_DIAG = 4
GATE_COLS = 8
CUMSUM_BLOCK = MXU_DIM
VMEM_LIMIT = 56 * 1024 * 1024


def _rmsnorm(x, g):
    ms = jnp.mean(x * x, axis=-1, keepdims=True)
    return x * lax.rsqrt(ms + EPS) * g


def _const_spec(shape):
    zeros = (0,) * len(shape)
    return pl.BlockSpec(shape, lambda *_: zeros, pipeline_mode=pl.Buffered(1))


def _memkv_kernel(mem_ref, g_ref, w_ref, o_ref):
    mn = _rmsnorm(mem_ref[0], g_ref[...]).astype(BF16)
    for l in range(w_ref.shape[0]):
        o_ref[l, 0] = jnp.dot(mn, w_ref[l], preferred_element_type=F32).astype(BF16)


def _memkv(mem, g_mem, w_mem_kv):
    bsz = mem.shape[0]
    depth = w_mem_kv.shape[0]
    return pl.pallas_call(
        _memkv_kernel,
        grid=(bsz,),
        in_specs=[
            pl.BlockSpec((1, MEM_TOKENS, D_MODEL), lambda b: (b, 0, 0)),
            _const_spec((1, D_MODEL)),
            _const_spec((depth, D_MODEL, 2 * D_MEM)),
        ],
        out_specs=pl.BlockSpec((depth, 1, MEM_TOKENS, 2 * D_MEM), lambda b: (0, b, 0, 0)),
        out_shape=jax.ShapeDtypeStruct((depth, bsz, MEM_TOKENS, 2 * D_MEM), BF16),
        compiler_params=pltpu.CompilerParams(dimension_semantics=("arbitrary",)),
        name="memkv",
    )(mem, g_mem.reshape(1, D_MODEL), w_mem_kv.astype(BF16))


def _s5_discretise(a_re, a_im, log_dt, b_re, b_im, c_re, c_im):
    dt = jnp.exp(log_dt.astype(F32))[:, None]
    lam_re = jnp.minimum(a_re.astype(F32), -1e-4)
    lam_im = a_im.astype(F32)
    mag = jnp.exp(lam_re * dt)
    ph = lam_im * dt
    ab_re, ab_im = mag * jnp.cos(ph), mag * jnp.sin(ph)
    den = lam_re * lam_re + lam_im * lam_im
    z_re = ((ab_re - 1.0) * lam_re + ab_im * lam_im) / den
    z_im = (ab_im * lam_re - (ab_re - 1.0) * lam_im) / den
    br, bi = b_re.astype(F32), b_im.astype(F32)
    bb_re = z_re[..., None] * br - z_im[..., None] * bi
    bb_im = z_re[..., None] * bi + z_im[..., None] * br
    gps = MXU_DIM // S5_GROUP
    eye = jnp.eye(gps, dtype=F32)

    def pack_b(bb):
        bb = bb.reshape(S5_SLABS, gps, S5_STATE, S5_GROUP)
        return jnp.einsum('sgpi,gh->sgihp', bb, eye).reshape(S5_SLABS, MXU_DIM, S5_SLAB_STATE)

    def pack_c(cc):
        cc = cc.reshape(S5_SLABS, gps, S5_GROUP, S5_STATE)
        return jnp.einsum('sgip,gh->sgphi', cc, eye).reshape(S5_SLABS, S5_SLAB_STATE, MXU_DIM)

    b_bd = jnp.concatenate([pack_b(bb_re), pack_b(bb_im)], axis=2).astype(BF16)
    c_bd = jnp.concatenate([pack_c(c_re.astype(F32)), -pack_c(c_im.astype(F32))], axis=1).astype(BF16)
    return ab_re.reshape(1, S5_STATE_ALL), ab_im.reshape(1, S5_STATE_ALL), b_bd, c_bd


def _s5_glu_kernel(x_ref, gmix_ref, win_ref, perm_ref, permt_ref, bbd_ref, cbd_ref, are_ref, aim_ref,
                   d_ref, wglu_ref, o_ref, qm_ref, hre_ref, him_ref, *bufs):
    rows = SUBLANES * S5_STEPS
    sre_refs, sim_refs, up_refs = bufs[0:S5_STAGES], bufs[S5_STAGES:2 * S5_STAGES], bufs[2 * S5_STAGES:]
    i = pl.program_id(0)

    @pl.when(i == 0)
    def _():
        hre_ref[...] = jnp.zeros_like(hre_ref)
        him_ref[...] = jnp.zeros_like(him_ref)
        for ref in bufs:
            ref[...] = jnp.zeros_like(ref)

    def permute_in(up_ref):
        xn = _rmsnorm(x_ref[...].reshape(rows, D_MODEL), gmix_ref[...]).astype(BF16)
        proj = jnp.dot(xn, win_ref[...], preferred_element_type=F32).astype(BF16)
        qm_ref[...] = proj[:, D_TOK:].reshape(SUBLANES, S5_STEPS, D_MEM)
        up_ref[...] = jnp.dot(perm_ref[...], proj[:, :D_TOK], preferred_element_type=F32).astype(BF16)

    def project_in(s, sre_ref, sim_ref, up_ref):
        bu = jnp.dot(up_ref[:, s * MXU_DIM:(s + 1) * MXU_DIM], bbd_ref[s], preferred_element_type=F32)
        sre_ref[:, s * S5_SLAB_STATE:(s + 1) * S5_SLAB_STATE] = bu[:, :S5_SLAB_STATE]
        sim_ref[:, s * S5_SLAB_STATE:(s + 1) * S5_SLAB_STATE] = bu[:, S5_SLAB_STATE:]

    def scan(c, sre_ref, sim_ref):
        cs = slice(c * SCAN_LANES, (c + 1) * SCAN_LANES)
        ar = jnp.broadcast_to(are_ref[:, cs], (SUBLANES, SCAN_LANES))
        ai = jnp.broadcast_to(aim_ref[:, cs], (SUBLANES, SCAN_LANES))
        hr = hre_ref[:, cs]
        hi = him_ref[:, cs]
        for t in range(S5_STEPS):
            rs = slice(t * SUBLANES, (t + 1) * SUBLANES)
            br = sre_ref[rs, cs]
            bi = sim_ref[rs, cs]
            hr, hi = ar * hr - ai * hi + br, ar * hi + ai * hr + bi
            sre_ref[rs, cs] = hr
            sim_ref[rs, cs] = hi
        hre_ref[:, cs] = hr
        him_ref[:, cs] = hi

    def project_out(s, sre_ref, sim_ref):
        ss = slice(s * S5_SLAB_STATE, (s + 1) * S5_SLAB_STATE)
        y_re = jnp.dot(sre_ref[:, ss].astype(BF16), cbd_ref[s, :S5_SLAB_STATE], preferred_element_type=F32)
        y_im = jnp.dot(sim_ref[:, ss].astype(BF16), cbd_ref[s, S5_SLAB_STATE:], preferred_element_type=F32)
        return y_re + y_im

    def gate_out(ys, up_ref):
        y = jnp.concatenate(ys, axis=1) + d_ref[...] * up_ref[...].astype(F32)
        toks = []
        for half in range(2):
            g = jax.nn.gelu(y[half * rows // 2:(half + 1) * rows // 2])
            z = jnp.dot(g.astype(BF16), wglu_ref[...], preferred_element_type=F32)
            toks.append((g * jax.nn.sigmoid(z)).astype(BF16))
        tok = jnp.concatenate(toks, axis=0)
        out = jnp.dot(permt_ref[...], tok, preferred_element_type=F32).astype(BF16)
        o_ref[...] = out.reshape(SUBLANES, S5_STEPS, D_TOK)

    scans_per_slab = S5_STATE_ALL // SCAN_LANES // S5_SLABS
    for r in range(S5_STAGES):
        @pl.when(i % S5_STAGES == r)
        def _(r=r):
            a, b = r, (r - 1) % S5_STAGES
            permute_in(up_refs[a])
            ys = []
            for s in range(S5_SLABS):
                project_in(s, sre_refs[a], sim_refs[a], up_refs[a])
                for k in range(scans_per_slab):
                    scan(s * scans_per_slab + k, sre_refs[b], sim_refs[b])
                ys.append(project_out(s, sre_refs[b], sim_refs[b]))
            gate_out(ys, up_refs[b])


def _s5_glu(h, g_mix, w_in, a_re, a_im, log_dt, b_re, b_im, c_re, c_im, d_skip, w_glu):
    bsz, seq, _ = h.shape
    assert bsz == SUBLANES, "the scan keeps one batch element per sublane"
    rows = SUBLANES * S5_STEPS
    n_blocks = seq // S5_STEPS
    lag = S5_STAGES - 1
    ab_re, ab_im, b_bd, c_bd = _s5_discretise(a_re, a_im, log_dt, b_re, b_im, c_re, c_im)
    src = (jnp.arange(rows) % SUBLANES) * S5_STEPS + jnp.arange(rows) // SUBLANES
    perm = (src[:, None] == jnp.arange(rows)[None, :]).astype(BF16)
    stage_bufs = ([pltpu.VMEM((rows, S5_STATE_ALL), F32)] * (2 * S5_STAGES)
                  + [pltpu.VMEM((rows, D_TOK), BF16)] * S5_STAGES)
    return pl.pallas_call(
        _s5_glu_kernel,
        grid=(n_blocks + lag,),
        in_specs=[
            pl.BlockSpec((bsz, S5_STEPS, D_MODEL), lambda i: (0, jnp.minimum(i, n_blocks - 1), 0)),
            _const_spec((1, D_MODEL)),
            _const_spec((D_MODEL, D_MODEL)),
            _const_spec((rows, rows)),
            _const_spec((rows, rows)),
            _const_spec((S5_SLABS, MXU_DIM, 2 * S5_SLAB_STATE)),
            _const_spec((S5_SLABS, 2 * S5_SLAB_STATE, MXU_DIM)),
            _const_spec((1, S5_STATE_ALL)),
            _const_spec((1, S5_STATE_ALL)),
            _const_spec((1, D_TOK)),
            _const_spec((D_TOK, D_TOK)),
        ],
        out_specs=[
            pl.BlockSpec((bsz, S5_STEPS, D_TOK), lambda i: (0, jnp.maximum(i - lag, 0), 0)),
            pl.BlockSpec((bsz, S5_STEPS, D_MEM), lambda i: (0, jnp.minimum(i, n_blocks - 1), 0)),
        ],
        out_shape=[
            jax.ShapeDtypeStruct((bsz, seq, D_TOK), BF16),
            jax.ShapeDtypeStruct((bsz, seq, D_MEM), BF16),
        ],
        scratch_shapes=[
            pltpu.VMEM((SUBLANES, S5_STATE_ALL), F32),
            pltpu.VMEM((SUBLANES, S5_STATE_ALL), F32),
        ] + stage_bufs,
        compiler_params=pltpu.CompilerParams(dimension_semantics=("arbitrary",),
                                             vmem_limit_bytes=VMEM_LIMIT),
        name="s5_glu",
    )(h, g_mix.reshape(1, D_MODEL), w_in.astype(BF16), perm, perm.T, b_bd, c_bd, ab_re, ab_im,
      d_skip.reshape(1, D_TOK).astype(F32), w_glu.astype(BF16))


def _split3(x):
    hi = x.astype(BF16)
    r = x - hi.astype(F32)
    mid = r.astype(BF16)
    lo = (r - mid.astype(F32)).astype(BF16)
    return hi, mid, lo


def _gate_columns(z, tri_ref, carry_ref):
    n_rows = z.shape[0]
    logf = -(jnp.maximum(-z, 0.0) + jnp.log1p(jnp.exp(-jnp.abs(z))))
    tri = tri_ref[...]
    parts = jnp.concatenate(_split3(logf), axis=1)
    running = carry_ref[...]
    blocks = []
    for r in range(n_rows // CUMSUM_BLOCK):
        c3 = jnp.dot(tri, parts[r * CUMSUM_BLOCK:(r + 1) * CUMSUM_BLOCK], preferred_element_type=F32)
        csum = c3[:, :LANES] + c3[:, LANES:2 * LANES] + c3[:, 2 * LANES:]
        blocks.append(csum + running)
        running = blocks[-1][CUMSUM_BLOCK - 1:CUMSUM_BLOCK, :]
    carry_ref[...] = running
    fcum = jnp.concatenate(blocks, axis=0)
    hi, mid, lo = (part.astype(F32) for part in _split3(fcum * LOG2E))
    lane = lax.broadcasted_iota(jnp.int32, fcum.shape, 1)
    j = lane & (GATE_COLS - 1)
    valid = lane < N_FOX_HEADS * GATE_COLS
    one = jnp.ones_like(hi)
    zero = jnp.zeros_like(hi)
    qb = jnp.where(j == 0, hi, jnp.where(j == 1, mid, jnp.where(j == 2, lo, jnp.where(j < 6, one, zero))))
    kb = jnp.where(j < 3, one, jnp.where(j == 3, -hi, jnp.where(j == 4, -mid, jnp.where(j == 5, -lo, zero))))
    return jnp.where(valid, qb, zero).astype(BF16), jnp.where(valid, kb, zero).astype(BF16)


def _gate_operands(w_fgate, b_fgate):
    pad = LANES - N_FOX_HEADS * GATE_COLS
    wf = jnp.pad(jnp.repeat(w_fgate, GATE_COLS, axis=1), ((0, 0), (0, pad))).astype(BF16)
    bfr = jnp.pad(jnp.repeat(b_fgate, GATE_COLS), (0, pad)).reshape(1, LANES).astype(F32)
    tri = (jnp.arange(CUMSUM_BLOCK)[:, None] >= jnp.arange(CUMSUM_BLOCK)[None, :]).astype(BF16)
    return wf, bfr, tri


def _fox_kernel(itab_ref, ktab_ref, q_ref, qb_ref, k_ref, v_ref, kb_ref, o_ref,
                qq_ref, m_ref, acc_ref, s_ref, p_ref, a_ref, *, kinds):
    hp = pl.program_id(1)
    seq = q_ref.shape[1]
    n_q = seq // FOX_TQ
    rows = 2 * FOX_TQ
    n_steps = len(kinds)

    def scores(t, kw):
        c0 = pl.multiple_of(ktab_ref[t], FOX_TQ)
        kk = jnp.concatenate([k_ref[0, pl.ds(c0, kw), :], kb_ref[0, pl.ds(c0, kw), :]], axis=1)
        return lax.dot_general(qq_ref[itab_ref[t]], kk, (((1,), (1,)), ((), ())), preferred_element_type=F32)

    def lane_blocks(s):
        return [s[:, n * LANES:(n + 1) * LANES] for n in range(s.shape[1] // LANES)]

    def row_max(blocks):
        mx = functools.reduce(jnp.maximum, blocks)
        return jnp.broadcast_to(jnp.max(mx, axis=1, keepdims=True), (rows, LANES))

    def probs(blocks, m):
        return jnp.concatenate([jnp.exp2(blk - m).astype(BF16) for blk in blocks], axis=1)

    def pv_dot(p, t, kw):
        c0 = pl.multiple_of(ktab_ref[t], FOX_TQ)
        vv = jnp.concatenate([v_ref[0, pl.ds(c0, kw), :], jnp.ones((kw, LANES), BF16)], axis=1)
        return jnp.dot(p, vv, preferred_element_type=F32)

    lane = lax.broadcasted_iota(jnp.int32, (FOX_TQ, LANES), 1)
    first = lane < HEAD_DIM
    gate_head = lane // GATE_COLS

    def build_q(i, carry):
        r0 = pl.multiple_of(i * FOX_TQ, FOX_TQ)
        q2 = q_ref[0, pl.ds(r0, FOX_TQ), :].astype(F32)
        qb = qb_ref[0, pl.ds(r0, FOX_TQ), :].astype(F32)
        zero = jnp.zeros_like(q2)
        qa = jnp.concatenate([jnp.where(first, q2, zero), jnp.where(gate_head == 2 * hp, qb, zero)], axis=1)
        qo = jnp.concatenate([jnp.where(first, zero, q2), jnp.where(gate_head == 2 * hp + 1, qb, zero)], axis=1)
        qq_ref[i] = jnp.concatenate([qa, qo], axis=0).astype(BF16)
        return carry

    lax.fori_loop(0, n_q, build_q, 0)

    visible = ((lax.broadcasted_iota(jnp.int32, (rows, FOX_TQ), 0) & (FOX_TQ - 1))
               >= lax.broadcasted_iota(jnp.int32, (rows, FOX_TQ), 1))

    def stage_scores(t, slot, kind):
        kw, _ = kind
        s_ref[slot, :, :kw] = scores(t, kw)

    def load_blocks(slot, kind):
        kw, masked = kind
        s = s_ref[slot, :, :kw]
        if masked:
            assert kw == FOX_TQ
            s = jnp.where(visible, s, -jnp.inf)
        return lane_blocks(s)

    def softmax_update(t, slot, kind, blocks):
        kw, first = kind
        i = itab_ref[t]
        if first:
            m_new = row_max(blocks)
        else:
            m_old = m_ref[i]
            m_new = jnp.maximum(m_old, row_max(blocks))
            a_ref[slot] = jnp.exp2(m_old - m_new)
        m_ref[i] = m_new
        p_ref[slot, :, :kw] = probs(blocks, m_new)

    def stage_pv(t, slot, kind):
        kw, first = kind
        i = itab_ref[t]
        pv = pv_dot(p_ref[slot, :, :kw], t, kw)
        if first:
            acc_ref[i] = pv
        else:
            alpha = a_ref[slot]
            acc_ref[i] = jnp.concatenate([alpha, alpha], axis=1) * acc_ref[i] + pv

    def tick(t, parity, kind2, kind1, kind0):
        assert parity in (0, 1)
        stage_scores(t + 2, parity, kind2)
        blocks = load_blocks(1 - parity, kind1)
        stage_pv(t, parity, kind0)
        softmax_update(t + 1, 1 - parity, kind1, blocks)

    stage_scores(0, 0, kinds[0])
    stage_scores(1, 1, kinds[1])
    softmax_update(0, 0, kinds[0], load_blocks(0, kinds[0]))
    t = 0
    while t < n_steps - 2:
        trio = (kinds[t + 2], kinds[t + 1], kinds[t])
        run = 1
        while t + run < n_steps - 2 and (kinds[t + run + 2], kinds[t + run + 1], kinds[t + run]) == trio:
            run += 1
        unroll = FOX_UNROLL if trio[0][0] == FOX_TK else FOX_UNROLL_DIAG
        looped = (run // unroll) * unroll if t % 2 == 0 else 0
        if looped:
            def body(u, carry, t=t, unroll=unroll, trio=trio):
                for k in range(unroll):
                    tick(t + unroll * u + k, k % 2, *trio)
                return carry

            lax.fori_loop(0, looped // unroll, body, 0)
        for k in range(looped, run):
            tick(t + k, (t + k) % 2, *trio)
        t += run
    last = n_steps - 1
    softmax_update(last, last % 2, kinds[last], load_blocks(last % 2, kinds[last]))
    stage_pv(last - 1, (last - 1) % 2, kinds[last - 1])
    stage_pv(last, last % 2, kinds[last])

    def finish(i, carry):
        acc = acc_ref[i]
        o = acc[:, :LANES] / acc[:, LANES:]
        r0 = pl.multiple_of(i * FOX_TQ, FOX_TQ)
        o_ref[0, pl.ds(r0, FOX_TQ), :] = jnp.where(first, o[:FOX_TQ], o[FOX_TQ:]).astype(BF16)
        return carry

    lax.fori_loop(0, n_q, finish, 0)


def _fox(proj, kv, qb, kb):
    bsz, seq, _ = proj.shape
    n_pairs = D_TOK // LANES
    n_q = seq // FOX_TQ
    rows = 2 * FOX_TQ
    steps = [(i, i * FOX_TQ, (FOX_TQ, True)) for i in range(n_q)]
    steps += [(i, (i * FOX_TQ) // FOX_TK * FOX_TK, (FOX_TQ, False))
              for i in range(n_q) if (i * FOX_TQ) % FOX_TK]
    steps += [(i, j * FOX_TK, (FOX_TK, False))
              for j in range(seq // FOX_TK) for i in range(n_q) if (i * FOX_TQ) // FOX_TK > j]
    assert sum(kw for _, _, (kw, _) in steps) == sum((i + 1) * FOX_TQ for i in range(n_q))
    assert [st[0] for st in steps[:n_q]] == list(range(n_q)) and all(st[2][1] for st in steps[:n_q])
    assert not any(st[2][1] for st in steps[n_q:])
    itab = jnp.asarray([st[0] for st in steps], jnp.int32)
    ktab = jnp.asarray([st[1] for st in steps], jnp.int32)
    grid_spec = pltpu.PrefetchScalarGridSpec(
        num_scalar_prefetch=2,
        grid=(bsz, n_pairs),
        in_specs=[
            pl.BlockSpec((1, seq, LANES), lambda b, p, *_: (b, 0, p)),
            pl.BlockSpec((1, seq, LANES), lambda b, p, *_: (b, 0, 0)),
            pl.BlockSpec((1, seq, LANES), lambda b, p, *_: (b, 0, p)),
            pl.BlockSpec((1, seq, LANES), lambda b, p, *_: (b, 0, n_pairs + p)),
            pl.BlockSpec((1, seq, LANES), lambda b, p, *_: (b, 0, 0)),
        ],
        out_specs=pl.BlockSpec((1, seq, LANES), lambda b, p, *_: (b, 0, p)),
        scratch_shapes=[
            pltpu.VMEM((n_q, rows, 2 * LANES), BF16),
            pltpu.VMEM((n_q, rows, LANES), F32),
            pltpu.VMEM((n_q, rows, 2 * LANES), F32),
            pltpu.VMEM((2, rows, FOX_TK), F32),
            pltpu.VMEM((2, rows, FOX_TK), BF16),
            pltpu.VMEM((2, rows, LANES), F32),
        ],
    )
    return pl.pallas_call(
        functools.partial(_fox_kernel, kinds=tuple(st[2] for st in steps)),
        grid_spec=grid_spec,
        out_shape=jax.ShapeDtypeStruct((bsz, seq, D_TOK), BF16),
        compiler_params=pltpu.CompilerParams(dimension_semantics=("arbitrary", "arbitrary"),
                                             vmem_limit_bytes=VMEM_LIMIT),
        name="fox",
    )(itab, ktab, proj, qb, kv, kv, kb)


def _mix_ffn_kernel(*refs, final_norm, emit_next):
    (h_ref, tok_ref, qm_ref, mkv_ref, wout_ref, gffn_ref, wup_ref, cw_ref, wdn_ref, gfin_ref), refs = (
        refs[:10], refs[10:])
    if emit_next:
        (gmix_ref, win_ref, gkv_ref, wkvf_ref, bf_ref, tri_ref), refs = refs[:6], refs[6:]
        (o_ref, proj_ref, kv_ref, qb_ref, kb_ref), refs = refs[:5], refs[5:]
        carry_ref, acc_ref, xn_ref, up_ref, act_ref, fcarry_ref = refs
    else:
        o_ref, carry_ref, acc_ref, xn_ref, up_ref, act_ref = refs
    tm = h_ref.shape[1]

    qm = qm_ref[0].astype(F32) * ((D_MEM // N_MEM_HEADS) ** -0.5)
    km = mkv_ref[0, :, :D_MEM]
    vm = mkv_ref[0, :, D_MEM:]
    lane = lax.broadcasted_iota(jnp.int32, (tm, D_MEM), 1)
    head = [(lane >= hh * HEAD_DIM) & (lane < (hh + 1) * HEAD_DIM) for hh in range(N_MEM_HEADS)]
    qs = jnp.concatenate([jnp.where(sel, qm, jnp.zeros_like(qm)) for sel in head], axis=0).astype(BF16)
    s = lax.dot_general(qs, km, (((1,), (1,)), ((), ())), preferred_element_type=F32)
    p = jnp.exp(s - jnp.max(s, axis=1, keepdims=True))
    o = jnp.dot(p.astype(BF16), vm, preferred_element_type=F32) / jnp.sum(p, axis=1, keepdims=True)
    mem = jnp.zeros((tm, D_MEM), F32)
    for hh, sel in enumerate(head):
        mem = jnp.where(sel, o[hh * tm:(hh + 1) * tm], mem)

    cat = jnp.concatenate([tok_ref[0], mem.astype(BF16)], axis=1)

    @pl.when(pl.program_id(1) == 0)
    def _():
        carry_ref[...] = jnp.zeros_like(carry_ref)
        if emit_next:
            fcarry_ref[...] = jnp.zeros_like(fcarry_ref)

    def stage_up(c, row_blocks=1):
        for r in range(row_blocks):
            rs = slice(r * tm // row_blocks, (r + 1) * tm // row_blocks)
            for half in range(2):
                cols = slice(half * D_FF + c * FF_CHUNK, half * D_FF + (c + 1) * FF_CHUNK)
                up_ref[c % 2, rs, half * FF_CHUNK:(half + 1) * FF_CHUNK] = jnp.dot(
                    xn_ref[rs], wup_ref[:, cols], preferred_element_type=F32)

    head_blocks = 2
    for r in range(head_blocks):
        rs = slice(r * tm // head_blocks, (r + 1) * tm // head_blocks)
        hmid = h_ref[0, rs] + jnp.dot(cat[rs], wout_ref[...], preferred_element_type=F32)
        acc_ref[rs] = hmid
        xn_ref[rs] = _rmsnorm(hmid, gffn_ref[...]).astype(BF16)

    row8 = lax.broadcasted_iota(jnp.int32, (SUBLANES, FF_CHUNK), 0)
    stage_up(0, head_blocks)
    for c in range(N_FF_CHUNKS):
        if c + 1 < N_FF_CHUNKS:
            stage_up(c + 1)
        a = up_ref[c % 2, :, :FF_CHUNK]
        g = up_ref[c % 2, :, FF_CHUNK:]
        prev = carry_ref[c]
        carry_ref[c] = g[tm - SUBLANES:, :]
        g1 = pltpu.roll(g, 1, 0)
        g2 = pltpu.roll(g, 2, 0)
        g1 = jnp.concatenate([jnp.where(row8 < 1, pltpu.roll(prev, 1, 0), g1[:SUBLANES]), g1[SUBLANES:]], axis=0)
        g2 = jnp.concatenate([jnp.where(row8 < 2, pltpu.roll(prev, 2, 0), g2[:SUBLANES]), g2[SUBLANES:]], axis=0)
        cw = cw_ref[:, c * FF_CHUNK:(c + 1) * FF_CHUNK]
        gc = g2 * cw[0:1] + g1 * cw[1:2] + g * cw[2:3] + cw[3:4]
        act_ref[:, c * FF_CHUNK:(c + 1) * FF_CHUNK] = (gc * jax.nn.sigmoid(gc) * a).astype(BF16)

    out = acc_ref[...] + jnp.dot(act_ref[...], wdn_ref[...], preferred_element_type=F32)
    if final_norm:
        out = _rmsnorm(out, gfin_ref[...])
    o_ref[0] = out
    if emit_next:
        normed = out * lax.rsqrt(jnp.mean(out * out, axis=-1, keepdims=True) + EPS)
        xn_next = (normed * gmix_ref[...]).astype(BF16)
        proj_ref[0] = jnp.dot(xn_next, win_ref[...], preferred_element_type=F32).astype(BF16)
        hs = (normed * gkv_ref[...]).astype(BF16)
        kvz = jnp.dot(hs, wkvf_ref[...], preferred_element_type=F32)
        kv_ref[0] = kvz[:, :2 * D_TOK].astype(BF16)
        qb_ref[0], kb_ref[0] = _gate_columns(kvz[:, 2 * D_TOK:] + bf_ref[...], tri_ref, fcarry_ref)


def _mix_ffn(h, tok, qmem, memkv, w_out, g_ffn, w_up, conv_w, conv_b, w_down, g_final, final_norm,
             next_layer=None):
    bsz, seq, _ = h.shape
    qmem_block = qmem.shape[-1] // D_MEM - 1
    tm = ROW_TILE
    wup = w_up.astype(BF16)
    wdn = w_down.astype(BF16)
    cw = jnp.concatenate([conv_w, conv_b[None, :], jnp.zeros((SUBLANES - 4, D_FF), conv_w.dtype)],
                         axis=0).astype(F32)
    def row_spec(width):
        return pl.BlockSpec((1, tm, width), lambda b, t: (b, t, 0))

    operands = [h, tok, qmem, memkv, w_out.astype(BF16), g_ffn.reshape(1, D_MODEL), wup, cw, wdn,
                g_final.reshape(1, D_MODEL)]
    in_specs = [
        row_spec(D_MODEL),
        row_spec(D_TOK),
        pl.BlockSpec((1, tm, D_MEM), lambda b, t: (b, t, qmem_block)),
        pl.BlockSpec((1, MEM_TOKENS, 2 * D_MEM), lambda b, t: (b, 0, 0)),
        _const_spec((D_MODEL, D_MODEL)),
        _const_spec((1, D_MODEL)),
        _const_spec((D_MODEL, 2 * D_FF)),
        _const_spec((SUBLANES, D_FF)),
        _const_spec((D_FF, D_MODEL)),
        _const_spec((1, D_MODEL)),
    ]
    out_specs = [row_spec(D_MODEL)]
    out_shape = [jax.ShapeDtypeStruct((bsz, seq, D_MODEL), F32)]
    scratch_shapes = [
        pltpu.VMEM((N_FF_CHUNKS, SUBLANES, FF_CHUNK), F32),
        pltpu.VMEM((tm, D_MODEL), F32),
        pltpu.VMEM((tm, D_MODEL), BF16),
        pltpu.VMEM((2, tm, 2 * FF_CHUNK), F32),
        pltpu.VMEM((tm, D_FF), BF16),
    ]
    if next_layer is not None:
        g_mix_n, w_in_n, g_kv, w_kv, w_fgate, b_fgate = next_layer
        wf, bfr, tri = _gate_operands(w_fgate, b_fgate)
        wkvf = jnp.concatenate([w_kv.astype(BF16), wf], axis=1)
        operands += [g_mix_n.reshape(1, D_MODEL), w_in_n.astype(BF16), g_kv.reshape(1, D_MODEL),
                     wkvf, bfr, tri]
        in_specs += [_const_spec((1, D_MODEL)), _const_spec((D_MODEL, D_MODEL)), _const_spec((1, D_MODEL)),
                     _const_spec((D_MODEL, 2 * D_TOK + LANES)), _const_spec((1, LANES)),
                     _const_spec((CUMSUM_BLOCK, CUMSUM_BLOCK))]
        out_specs += [row_spec(D_MODEL), row_spec(2 * D_TOK), row_spec(LANES), row_spec(LANES)]
        out_shape += [jax.ShapeDtypeStruct((bsz, seq, n), BF16) for n in (D_MODEL, 2 * D_TOK, LANES, LANES)]
        scratch_shapes.append(pltpu.VMEM((1, LANES), F32))
    outs = pl.pallas_call(
        functools.partial(_mix_ffn_kernel, final_norm=final_norm, emit_next=next_layer is not None),
        grid=(bsz, seq // tm),
        in_specs=in_specs,
        out_specs=out_specs,
        out_shape=out_shape,
        scratch_shapes=scratch_shapes,
        compiler_params=pltpu.CompilerParams(dimension_semantics=("arbitrary", "arbitrary"),
                                             vmem_limit_bytes=VMEM_LIMIT),
        name="mix_ffn",
    )(*operands)
    return outs if next_layer is not None else outs[0]


def kernel(x, mem, g_mix, w_in, w_out, g_mem, w_mem_kv, s5_a_re, s5_a_im, s5_log_dt, s5_b_re, s5_b_im,
           s5_c_re, s5_c_im, s5_d, w_glu, g_kv, w_kv, w_fgate, b_fgate, g_ffn, w_ffn_up, conv_w, conv_b,
           w_ffn_down, g_final):
    depth = w_in.shape[0]
    assert depth == 2, "one S5 layer followed by one attention layer"
    memkv = _memkv(mem, g_mem, w_mem_kv)

    def ffn_args(l):
        return (memkv[l], w_out[l], g_ffn[l], w_ffn_up[l], conv_w[l], conv_b[l], w_ffn_down[l], g_final)

    tok, qmem = _s5_glu(x, g_mix[0], w_in[0], s5_a_re[0], s5_a_im[0], s5_log_dt[0], s5_b_re[0],
                        s5_b_im[0], s5_c_re[0], s5_c_im[0], s5_d[0], w_glu[0])
    q_scale = jnp.where(jnp.arange(D_MODEL) < D_TOK, (HEAD_DIM ** -0.5) * LOG2E, 1.0).astype(F32)
    h, proj, kv, qb, kb = _mix_ffn(x, tok, qmem, *ffn_args(0), final_norm=False,
                                   next_layer=(g_mix[1], w_in[1] * q_scale[None, :], g_kv, w_kv,
                                               w_fgate, b_fgate))
    tok = _fox(proj, kv, qb, kb)
    return _mix_ffn(h, tok, proj, *ffn_args(1), final_norm=True)
```

```python
import functools
import math

import jax
import jax.numpy as jnp
from jax import lax
from jax.experimental import pallas as pl
from jax.experimental.pallas import tpu as pltpu

F32 = jnp.float32
BF16 = jnp.bfloat16

D_MODEL = 1024
HEAD_DIM = 64
D_MEM = 256
N_MEM_HEADS = 4
MEM_TOKENS = 256
D_TOK = D_MODEL - D_MEM
S5_GROUP = 16
S5_GROUPS = D_TOK // S5_GROUP
S5_STATE = 64
N_FOX_HEADS = D_TOK // HEAD_DIM
D_FF = 2816
EPS = 1e-6
LOG2E = math.log2(math.e)

LANES = 128
SUBLANES = 8
MXU_DIM = 256

ROW_TILE = 512
S5_STEPS = 64
S5_SLABS = D_TOK // MXU_DIM
S5_SLAB_STATE = (MXU_DIM // S5_GROUP) * S5_STATE
S5_STATE_ALL = S5_GROUPS * S5_STATE
S5_STAGES = 2
SCAN_LANES = 512
FF_CHUNK = 256
N_FF_CHUNKS = D_FF // FF_CHUNK
FOX_TQ = 256
FOX_TK = 512
FOX_UNROLL = 16
FOX_UNROLL_DIAG = 4
GATE_COLS = 8
CUMSUM_BLOCK = MXU_DIM
VMEM_LIMIT = 56 * 1024 * 1024


def _rmsnorm(x, g):
    ms = jnp.mean(x * x, axis=-1, keepdims=True)
    return x * lax.rsqrt(ms + EPS) * g


def _const_spec(shape):
    zeros = (0,) * len(shape)
    return pl.BlockSpec(shape, lambda *_: zeros, pipeline_mode=pl.Buffered(1))


def _memkv_kernel(mem_ref, g_ref, w_ref, o_ref):
    mn = _rmsnorm(mem_ref[0], g_ref[...]).astype(BF16)
    for l in range(w_ref.shape[0]):
        o_ref[l, 0] = jnp.dot(mn, w_ref[l], preferred_element_type=F32).astype(BF16)


def _memkv(mem, g_mem, w_mem_kv):
    bsz = mem.shape[0]
    depth = w_mem_kv.shape[0]
    return pl.pallas_call(
        _memkv_kernel,
        grid=(bsz,),
        in_specs=[
            pl.BlockSpec((1, MEM_TOKENS, D_MODEL), lambda b: (b, 0, 0)),
            _const_spec((1, D_MODEL)),
            _const_spec((depth, D_MODEL, 2 * D_MEM)),
        ],
        out_specs=pl.BlockSpec((depth, 1, MEM_TOKENS, 2 * D_MEM), lambda b: (0, b, 0, 0)),
        out_shape=jax.ShapeDtypeStruct((depth, bsz, MEM_TOKENS, 2 * D_MEM), BF16),
        compiler_params=pltpu.CompilerParams(dimension_semantics=("arbitrary",)),
        name="memkv",
    )(mem, g_mem.reshape(1, D_MODEL), w_mem_kv.astype(BF16))


def _s5_discretise(a_re, a_im, log_dt, b_re, b_im, c_re, c_im):
    dt = jnp.exp(log_dt.astype(F32))[:, None]
    lam_re = jnp.minimum(a_re.astype(F32), -1e-4)
    lam_im = a_im.astype(F32)
    mag = jnp.exp(lam_re * dt)
    ph = lam_im * dt
    ab_re, ab_im = mag * jnp.cos(ph), mag * jnp.sin(ph)
    den = lam_re * lam_re + lam_im * lam_im
    z_re = ((ab_re - 1.0) * lam_re + ab_im * lam_im) / den
    z_im = (ab_im * lam_re - (ab_re - 1.0) * lam_im) / den
    br, bi = b_re.astype(F32), b_im.astype(F32)
    bb_re = z_re[..., None] * br - z_im[..., None] * bi
    bb_im = z_re[..., None] * bi + z_im[..., None] * br
    gps = MXU_DIM // S5_GROUP
    eye = jnp.eye(gps, dtype=F32)

    def pack_b(bb):
        bb = bb.reshape(S5_SLABS, gps, S5_STATE, S5_GROUP)
        return jnp.einsum('sgpi,gh->sgihp', bb, eye).reshape(S5_SLABS, MXU_DIM, S5_SLAB_STATE)

    def pack_c(cc):
        cc = cc.reshape(S5_SLABS, gps, S5_GROUP, S5_STATE)
        return jnp.einsum('sgip,gh->sgphi', cc, eye).reshape(S5_SLABS, S5_SLAB_STATE, MXU_DIM)

    b_bd = jnp.concatenate([pack_b(bb_re), pack_b(bb_im)], axis=2).astype(BF16)
    c_bd = jnp.concatenate([pack_c(c_re.astype(F32)), -pack_c(c_im.astype(F32))], axis=1).astype(BF16)
    return ab_re.reshape(1, S5_STATE_ALL), ab_im.reshape(1, S5_STATE_ALL), b_bd, c_bd


def _s5_glu_kernel(x_ref, gmix_ref, win_ref, perm_ref, permt_ref, bbd_ref, cbd_ref, are_ref, aim_ref,
                   d_ref, wglu_ref, o_ref, qm_ref, hre_ref, him_ref, *bufs):
    rows = SUBLANES * S5_STEPS
    sre_refs, sim_refs, up_refs = bufs[0:S5_STAGES], bufs[S5_STAGES:2 * S5_STAGES], bufs[2 * S5_STAGES:]
    i = pl.program_id(0)

    @pl.when(i == 0)
    def _():
        hre_ref[...] = jnp.zeros_like(hre_ref)
        him_ref[...] = jnp.zeros_like(him_ref)
        for ref in bufs:
            ref[...] = jnp.zeros_like(ref)

    def permute_in(up_ref):
        xn = _rmsnorm(x_ref[...].reshape(rows, D_MODEL), gmix_ref[...]).astype(BF16)
        proj = jnp.dot(xn, win_ref[...], preferred_element_type=F32).astype(BF16)
        qm_ref[...] = proj[:, D_TOK:].reshape(SUBLANES, S5_STEPS, D_MEM)
        up_ref[...] = jnp.dot(perm_ref[...], proj[:, :D_TOK], preferred_element_type=F32).astype(BF16)

    def project_in(s, sre_ref, sim_ref, up_ref):
        bu = jnp.dot(up_ref[:, s * MXU_DIM:(s + 1) * MXU_DIM], bbd_ref[s], preferred_element_type=F32)
        sre_ref[:, s * S5_SLAB_STATE:(s + 1) * S5_SLAB_STATE] = bu[:, :S5_SLAB_STATE]
        sim_ref[:, s * S5_SLAB_STATE:(s + 1) * S5_SLAB_STATE] = bu[:, S5_SLAB_STATE:]

    def scan(c, sre_ref, sim_ref):
        cs = slice(c * SCAN_LANES, (c + 1) * SCAN_LANES)
        ar = jnp.broadcast_to(are_ref[:, cs], (SUBLANES, SCAN_LANES))
        ai = jnp.broadcast_to(aim_ref[:, cs], (SUBLANES, SCAN_LANES))
        hr = hre_ref[:, cs]
        hi = him_ref[:, cs]
        for t in range(S5_STEPS):
            rs = slice(t * SUBLANES, (t + 1) * SUBLANES)
            br = sre_ref[rs, cs]
            bi = sim_ref[rs, cs]
            hr, hi = ar * hr - ai * hi + br, ar * hi + ai * hr + bi
            sre_ref[rs, cs] = hr
            sim_ref[rs, cs] = hi
        hre_ref[:, cs] = hr
        him_ref[:, cs] = hi

    def project_out(s, sre_ref, sim_ref):
        ss = slice(s * S5_SLAB_STATE, (s + 1) * S5_SLAB_STATE)
        y_re = jnp.dot(sre_ref[:, ss].astype(BF16), cbd_ref[s, :S5_SLAB_STATE], preferred_element_type=F32)
        y_im = jnp.dot(sim_ref[:, ss].astype(BF16), cbd_ref[s, S5_SLAB_STATE:], preferred_element_type=F32)
        return y_re + y_im

    def gate_out(ys, up_ref):
        y = jnp.concatenate(ys, axis=1) + d_ref[...] * up_ref[...].astype(F32)
        toks = []
        for half in range(2):
            g = jax.nn.gelu(y[half * rows // 2:(half + 1) * rows // 2])
            z = jnp.dot(g.astype(BF16), wglu_ref[...], preferred_element_type=F32)
            toks.append((g * jax.nn.sigmoid(z)).astype(BF16))
        tok = jnp.concatenate(toks, axis=0)
        out = jnp.dot(permt_ref[...], tok, preferred_element_type=F32).astype(BF16)
        o_ref[...] = out.reshape(SUBLANES, S5_STEPS, D_TOK)

    scans_per_slab = S5_STATE_ALL // SCAN_LANES // S5_SLABS
    for r in range(S5_STAGES):
        @pl.when(i % S5_STAGES == r)
        def _(r=r):
            a, b = r, (r - 1) % S5_STAGES
            permute_in(up_refs[a])
            ys = []
            for s in range(S5_SLABS):
                project_in(s, sre_refs[a], sim_refs[a], up_refs[a])
                for k in range(scans_per_slab):
                    scan(s * scans_per_slab + k, sre_refs[b], sim_refs[b])
                ys.append(project_out(s, sre_refs[b], sim_refs[b]))
            gate_out(ys, up_refs[b])


def _s5_glu(h, g_mix, w_in, a_re, a_im, log_dt, b_re, b_im, c_re, c_im, d_skip, w_glu):
    bsz, seq, _ = h.shape
    assert bsz == SUBLANES, "the scan keeps one batch element per sublane"
    rows = SUBLANES * S5_STEPS
    n_blocks = seq // S5_STEPS
    lag = S5_STAGES - 1
    ab_re, ab_im, b_bd, c_bd = _s5_discretise(a_re, a_im, log_dt, b_re, b_im, c_re, c_im)
    src = (jnp.arange(rows) % SUBLANES) * S5_STEPS + jnp.arange(rows) // SUBLANES
    perm = (src[:, None] == jnp.arange(rows)[None, :]).astype(BF16)
    stage_bufs = ([pltpu.VMEM((rows, S5_STATE_ALL), F32)] * (2 * S5_STAGES)
                  + [pltpu.VMEM((rows, D_TOK), BF16)] * S5_STAGES)
    return pl.pallas_call(
        _s5_glu_kernel,
        grid=(n_blocks + lag,),
        in_specs=[
            pl.BlockSpec((bsz, S5_STEPS, D_MODEL), lambda i: (0, jnp.minimum(i, n_blocks - 1), 0)),
            _const_spec((1, D_MODEL)),
            _const_spec((D_MODEL, D_MODEL)),
            _const_spec((rows, rows)),
            _const_spec((rows, rows)),
            _const_spec((S5_SLABS, MXU_DIM, 2 * S5_SLAB_STATE)),
            _const_spec((S5_SLABS, 2 * S5_SLAB_STATE, MXU_DIM)),
            _const_spec((1, S5_STATE_ALL)),
            _const_spec((1, S5_STATE_ALL)),
            _const_spec((1, D_TOK)),
            _const_spec((D_TOK, D_TOK)),
        ],
        out_specs=[
            pl.BlockSpec((bsz, S5_STEPS, D_TOK), lambda i: (0, jnp.maximum(i - lag, 0), 0)),
            pl.BlockSpec((bsz, S5_STEPS, D_MEM), lambda i: (0, jnp.minimum(i, n_blocks - 1), 0)),
        ],
        out_shape=[
            jax.ShapeDtypeStruct((bsz, seq, D_TOK), BF16),
            jax.ShapeDtypeStruct((bsz, seq, D_MEM), BF16),
        ],
        scratch_shapes=[
            pltpu.VMEM((SUBLANES, S5_STATE_ALL), F32),
            pltpu.VMEM((SUBLANES, S5_STATE_ALL), F32),
        ] + stage_bufs,
        compiler_params=pltpu.CompilerParams(dimension_semantics=("arbitrary",),
                                             vmem_limit_bytes=VMEM_LIMIT),
        name="s5_glu",
    )(h, g_mix.reshape(1, D_MODEL), w_in.astype(BF16), perm, perm.T, b_bd, c_bd, ab_re, ab_im,
      d_skip.reshape(1, D_TOK).astype(F32), w_glu.astype(BF16))


def _split3(x):
    hi = x.astype(BF16)
    r = x - hi.astype(F32)
    mid = r.astype(BF16)
    lo = (r - mid.astype(F32)).astype(BF16)
    return hi, mid, lo


def _gate_columns(z, tri_ref, carry_ref):
    n_rows = z.shape[0]
    logf = -(jnp.maximum(-z, 0.0) + jnp.log1p(jnp.exp(-jnp.abs(z))))
    tri = tri_ref[...]
    parts = jnp.concatenate(_split3(logf), axis=1)
    running = carry_ref[...]
    blocks = []
    for r in range(n_rows // CUMSUM_BLOCK):
        c3 = jnp.dot(tri, parts[r * CUMSUM_BLOCK:(r + 1) * CUMSUM_BLOCK], preferred_element_type=F32)
        csum = c3[:, :LANES] + c3[:, LANES:2 * LANES] + c3[:, 2 * LANES:]
        blocks.append(csum + running)
        running = blocks[-1][CUMSUM_BLOCK - 1:CUMSUM_BLOCK, :]
    carry_ref[...] = running
    fcum = jnp.concatenate(blocks, axis=0)
    hi, mid, lo = (part.astype(F32) for part in _split3(fcum * LOG2E))
    lane = lax.broadcasted_iota(jnp.int32, fcum.shape, 1)
    j = lane & (GATE_COLS - 1)
    valid = lane < N_FOX_HEADS * GATE_COLS
    one = jnp.ones_like(hi)
    zero = jnp.zeros_like(hi)
    qb = jnp.where(j == 0, hi, jnp.where(j == 1, mid, jnp.where(j == 2, lo, jnp.where(j < 6, one, zero))))
    kb = jnp.where(j < 3, one, jnp.where(j == 3, -hi, jnp.where(j == 4, -mid, jnp.where(j == 5, -lo, zero))))
    return jnp.where(valid, qb, zero).astype(BF16), jnp.where(valid, kb, zero).astype(BF16)


def _gate_operands(w_fgate, b_fgate):
    pad = LANES - N_FOX_HEADS * GATE_COLS
    wf = jnp.pad(jnp.repeat(w_fgate, GATE_COLS, axis=1), ((0, 0), (0, pad))).astype(BF16)
    bfr = jnp.pad(jnp.repeat(b_fgate, GATE_COLS), (0, pad)).reshape(1, LANES).astype(F32)
    tri = (jnp.arange(CUMSUM_BLOCK)[:, None] >= jnp.arange(CUMSUM_BLOCK)[None, :]).astype(BF16)
    return wf, bfr, tri


def _fox_kernel(itab_ref, ktab_ref, q_ref, qb_ref, k_ref, v_ref, kb_ref, o_ref,
                qq_ref, m_ref, acc_ref, s_ref, p_ref, a_ref, *, kinds):
    hp = pl.program_id(1)
    seq = q_ref.shape[1]
    n_q = seq // FOX_TQ
    rows = 2 * FOX_TQ
    n_steps = len(kinds)

    def scores(t, kw):
        c0 = pl.multiple_of(ktab_ref[t], FOX_TQ)
        kk = jnp.concatenate([k_ref[0, pl.ds(c0, kw), :], kb_ref[0, pl.ds(c0, kw), :]], axis=1)
        return lax.dot_general(qq_ref[itab_ref[t]], kk, (((1,), (1,)), ((), ())), preferred_element_type=F32)

    def lane_blocks(s):
        return [s[:, n * LANES:(n + 1) * LANES] for n in range(s.shape[1] // LANES)]

    def row_max(blocks):
        mx = functools.reduce(jnp.maximum, blocks)
        return jnp.broadcast_to(jnp.max(mx, axis=1, keepdims=True), (rows, LANES))

    def probs(blocks, m):
        return jnp.concatenate([jnp.exp2(blk - m).astype(BF16) for blk in blocks], axis=1)

    def pv_dot(p, t, kw):
        c0 = pl.multiple_of(ktab_ref[t], FOX_TQ)
        vv = jnp.concatenate([v_ref[0, pl.ds(c0, kw), :], jnp.ones((kw, LANES), BF16)], axis=1)
        return jnp.dot(p, vv, preferred_element_type=F32)

    lane = lax.broadcasted_iota(jnp.int32, (FOX_TQ, LANES), 1)
    first = lane < HEAD_DIM
    gate_head = lane // GATE_COLS

    def build_q(i, carry):
        r0 = pl.multiple_of(i * FOX_TQ, FOX_TQ)
        q2 = q_ref[0, pl.ds(r0, FOX_TQ), :].astype(F32)
        qb = qb_ref[0, pl.ds(r0, FOX_TQ), :].astype(F32)
        zero = jnp.zeros_like(q2)
        qa = jnp.concatenate([jnp.where(first, q2, zero), jnp.where(gate_head == 2 * hp, qb, zero)], axis=1)
        qo = jnp.concatenate([jnp.where(first, zero, q2), jnp.where(gate_head == 2 * hp + 1, qb, zero)], axis=1)
        qq_ref[i] = jnp.concatenate([qa, qo], axis=0).astype(BF16)
        return carry

    lax.fori_loop(0, n_q, build_q, 0)

    visible = ((lax.broadcasted_iota(jnp.int32, (rows, FOX_TQ), 0) & (FOX_TQ - 1))
               >= lax.broadcasted_iota(jnp.int32, (rows, FOX_TQ), 1))

    def stage_scores(t, slot, kind):
        kw, _ = kind
        s_ref[slot, :, :kw] = scores(t, kw)

    def load_blocks(slot, kind):
        kw, masked = kind
        s = s_ref[slot, :, :kw]
        if masked:
            assert kw == FOX_TQ
            s = jnp.where(visible, s, -jnp.inf)
        return lane_blocks(s)

    def softmax_update(t, slot, kind, blocks):
        kw, first = kind
        i = itab_ref[t]
        if first:
            m_new = row_max(blocks)
        else:
            m_old = m_ref[i]
            m_new = jnp.maximum(m_old, row_max(blocks))
            a_ref[slot] = jnp.exp2(m_old - m_new)
        m_ref[i] = m_new
        p_ref[slot, :, :kw] = probs(blocks, m_new)

    def stage_pv(t, slot, kind):
        kw, first = kind
        i = itab_ref[t]
        pv = pv_dot(p_ref[slot, :, :kw], t, kw)
        if first:
            acc_ref[i] = pv
        else:
            alpha = a_ref[slot]
            acc_ref[i] = jnp.concatenate([alpha, alpha], axis=1) * acc_ref[i] + pv

    def tick(t, parity, kind2, kind1, kind0):
        assert parity in (0, 1)
        stage_scores(t + 2, parity, kind2)
        blocks = load_blocks(1 - parity, kind1)
        stage_pv(t, parity, kind0)
        softmax_update(t + 1, 1 - parity, kind1, blocks)

    stage_scores(0, 0, kinds[0])
    stage_scores(1, 1, kinds[1])
    softmax_update(0, 0, kinds[0], load_blocks(0, kinds[0]))
    t = 0
    while t < n_steps - 2:
        trio = (kinds[t + 2], kinds[t + 1], kinds[t])
        run = 1
        while t + run < n_steps - 2 and (kinds[t + run + 2], kinds[t + run + 1], kinds[t + run]) == trio:
            run += 1
        unroll = FOX_UNROLL if trio[0][0] == FOX_TK else FOX_UNROLL_DIAG
        looped = (run // unroll) * unroll if t % 2 == 0 else 0
        if looped:
            def body(u, carry, t=t, unroll=unroll, trio=trio):
                for k in range(unroll):
                    tick(t + unroll * u + k, k % 2, *trio)
                return carry

            lax.fori_loop(0, looped // unroll, body, 0)
        for k in range(looped, run):
            tick(t + k, (t + k) % 2, *trio)
        t += run
    last = n_steps - 1
    softmax_update(last, last % 2, kinds[last], load_blocks(last % 2, kinds[last]))
    stage_pv(last - 1, (last - 1) % 2, kinds[last - 1])
    stage_pv(last, last % 2, kinds[last])

    def finish(i, carry):
        acc = acc_ref[i]
        o = acc[:, :LANES] / acc[:, LANES:]
        r0 = pl.multiple_of(i * FOX_TQ, FOX_TQ)
        o_ref[0, pl.ds(r0, FOX_TQ), :] = jnp.where(first, o[:FOX_TQ], o[FOX_TQ:]).astype(BF16)
        return carry

    lax.fori_loop(0, n_q, finish, 0)


def _fox(proj, kv, qb, kb):
    bsz, seq, _ = proj.shape
    n_pairs = D_TOK // LANES
    n_q = seq // FOX_TQ
    rows = 2 * FOX_TQ
    steps = [(i, i * FOX_TQ, (FOX_TQ, True)) for i in range(n_q)]
    steps += [(i, (i * FOX_TQ) // FOX_TK * FOX_TK, (FOX_TQ, False))
              for i in range(n_q) if (i * FOX_TQ) % FOX_TK]
    steps += [(i, j * FOX_TK, (FOX_TK, False))
              for j in range(seq // FOX_TK) for i in range(n_q) if (i * FOX_TQ) // FOX_TK > j]
    assert sum(kw for _, _, (kw, _) in steps) == sum((i + 1) * FOX_TQ for i in range(n_q))
    assert [st[0] for st in steps[:n_q]] == list(range(n_q)) and all(st[2][1] for st in steps[:n_q])
    assert not any(st[2][1] for st in steps[n_q:])
    itab = jnp.asarray([st[0] for st in steps], jnp.int32)
    ktab = jnp.asarray([st[1] for st in steps], jnp.int32)
    grid_spec = pltpu.PrefetchScalarGridSpec(
        num_scalar_prefetch=2,
        grid=(bsz, n_pairs),
        in_specs=[
            pl.BlockSpec((1, seq, LANES), lambda b, p, *_: (b, 0, p)),
            pl.BlockSpec((1, seq, LANES), lambda b, p, *_: (b, 0, 0)),
            pl.BlockSpec((1, seq, LANES), lambda b, p, *_: (b, 0, p)),
            pl.BlockSpec((1, seq, LANES), lambda b, p, *_: (b, 0, n_pairs + p)),
            pl.BlockSpec((1, seq, LANES), lambda b, p, *_: (b, 0, 0)),
        ],
        out_specs=pl.BlockSpec((1, seq, LANES), lambda b, p, *_: (b, 0, p)),
        scratch_shapes=[
            pltpu.VMEM((n_q, rows, 2 * LANES), BF16),
            pltpu.VMEM((n_q, rows, LANES), F32),
            pltpu.VMEM((n_q, rows, 2 * LANES), F32),
            pltpu.VMEM((2, rows, FOX_TK), F32),
            pltpu.VMEM((2, rows, FOX_TK), BF16),
            pltpu.VMEM((2, rows, LANES), F32),
        ],
    )
    return pl.pallas_call(
        functools.partial(_fox_kernel, kinds=tuple(st[2] for st in steps)),
        grid_spec=grid_spec,
        out_shape=jax.ShapeDtypeStruct((bsz, seq, D_TOK), BF16),
        compiler_params=pltpu.CompilerParams(dimension_semantics=("arbitrary", "arbitrary"),
                                             vmem_limit_bytes=VMEM_LIMIT),
        name="fox",
    )(itab, ktab, proj, qb, kv, kv, kb)


def _mix_ffn_kernel(*refs, final_norm, emit_next):
    (h_ref, tok_ref, qm_ref, mkv_ref, wout_ref, gffn_ref, wup_ref, cw_ref, wdn_ref, gfin_ref), refs = (
        refs[:10], refs[10:])
    if emit_next:
        (gmix_ref, win_ref, gkv_ref, wkvf_ref, bf_ref, tri_ref), refs = refs[:6], refs[6:]
        (o_ref, proj_ref, kv_ref, qb_ref, kb_ref), refs = refs[:5], refs[5:]
        carry_ref, acc_ref, xn_ref, up_ref, act_ref, fcarry_ref = refs
    else:
        o_ref, carry_ref, acc_ref, xn_ref, up_ref, act_ref = refs
    tm = h_ref.shape[1]

    qm = qm_ref[0].astype(F32) * ((D_MEM // N_MEM_HEADS) ** -0.5)
    km = mkv_ref[0, :, :D_MEM]
    vm = mkv_ref[0, :, D_MEM:]
    lane = lax.broadcasted_iota(jnp.int32, (tm, D_MEM), 1)
    head = [(lane >= hh * HEAD_DIM) & (lane < (hh + 1) * HEAD_DIM) for hh in range(N_MEM_HEADS)]
    qs = jnp.concatenate([jnp.where(sel, qm, jnp.zeros_like(qm)) for sel in head], axis=0).astype(BF16)
    s = lax.dot_general(qs, km, (((1,), (1,)), ((), ())), preferred_element_type=F32)
    p = jnp.exp(s - jnp.max(s, axis=1, keepdims=True))
    o = jnp.dot(p.astype(BF16), vm, preferred_element_type=F32) / jnp.sum(p, axis=1, keepdims=True)
    mem = jnp.zeros((tm, D_MEM), F32)
    for hh, sel in enumerate(head):
        mem = jnp.where(sel, o[hh * tm:(hh + 1) * tm], mem)

    cat = jnp.concatenate([tok_ref[0], mem.astype(BF16)], axis=1)

    @pl.when(pl.program_id(1) == 0)
    def _():
        carry_ref[...] = jnp.zeros_like(carry_ref)
        if emit_next:
            fcarry_ref[...] = jnp.zeros_like(fcarry_ref)

    def stage_up(c, row_blocks=1):
        for r in range(row_blocks):
            rs = slice(r * tm // row_blocks, (r + 1) * tm // row_blocks)
            for half in range(2):
                cols = slice(half * D_FF + c * FF_CHUNK, half * D_FF + (c + 1) * FF_CHUNK)
                up_ref[c % 2, rs, half * FF_CHUNK:(half + 1) * FF_CHUNK] = jnp.dot(
                    xn_ref[rs], wup_ref[:, cols], preferred_element_type=F32)

    head_blocks = 2
    for r in range(head_blocks):
        rs = slice(r * tm // head_blocks, (r + 1) * tm // head_blocks)
        hmid = h_ref[0, rs] + jnp.dot(cat[rs], wout_ref[...], preferred_element_type=F32)
        acc_ref[rs] = hmid
        xn_ref[rs] = _rmsnorm(hmid, gffn_ref[...]).astype(BF16)

    row8 = lax.broadcasted_iota(jnp.int32, (SUBLANES, FF_CHUNK), 0)
    stage_up(0, head_blocks)
    for c in range(N_FF_CHUNKS):
        if c + 1 < N_FF_CHUNKS:
            stage_up(c + 1)
        a = up_ref[c % 2, :, :FF_CHUNK]
        g = up_ref[c % 2, :, FF_CHUNK:]
        prev = carry_ref[c]
        carry_ref[c] = g[tm - SUBLANES:, :]
        g1 = pltpu.roll(g, 1, 0)
        g2 = pltpu.roll(g, 2, 0)
        g1 = jnp.concatenate([jnp.where(row8 < 1, pltpu.roll(prev, 1, 0), g1[:SUBLANES]), g1[SUBLANES:]], axis=0)
        g2 = jnp.concatenate([jnp.where(row8 < 2, pltpu.roll(prev, 2, 0), g2[:SUBLANES]), g2[SUBLANES:]], axis=0)
        cw = cw_ref[:, c * FF_CHUNK:(c + 1) * FF_CHUNK]
        gc = g2 * cw[0:1] + g1 * cw[1:2] + g * cw[2:3] + cw[3:4]
        act_ref[:, c * FF_CHUNK:(c + 1) * FF_CHUNK] = (gc * jax.nn.sigmoid(gc) * a).astype(BF16)

    out = acc_ref[...] + jnp.dot(act_ref[...], wdn_ref[...], preferred_element_type=F32)
    if final_norm:
        out = _rmsnorm(out, gfin_ref[...])
    o_ref[0] = out
    if emit_next:
        normed = out * lax.rsqrt(jnp.mean(out * out, axis=-1, keepdims=True) + EPS)
        xn_next = (normed * gmix_ref[...]).astype(BF16)
        proj_ref[0] = jnp.dot(xn_next, win_ref[...], preferred_element_type=F32).astype(BF16)
        hs = (normed * gkv_ref[...]).astype(BF16)
        kvz = jnp.dot(hs, wkvf_ref[...], preferred_element_type=F32)
        kv_ref[0] = kvz[:, :2 * D_TOK].astype(BF16)
        qb_ref[0], kb_ref[0] = _gate_columns(kvz[:, 2 * D_TOK:] + bf_ref[...], tri_ref, fcarry_ref)


def _mix_ffn(h, tok, qmem, memkv, w_out, g_ffn, w_up, conv_w, conv_b, w_down, g_final, final_norm,
             next_layer=None):
    bsz, seq, _ = h.shape
    qmem_block = qmem.shape[-1] // D_MEM - 1
    tm = ROW_TILE
    wup = w_up.astype(BF16)
    wdn = w_down.astype(BF16)
    cw = jnp.concatenate([conv_w, conv_b[None, :], jnp.zeros((SUBLANES - 4, D_FF), conv_w.dtype)],
                         axis=0).astype(F32)
    def row_spec(width):
        return pl.BlockSpec((1, tm, width), lambda b, t: (b, t, 0))

    operands = [h, tok, qmem, memkv, w_out.astype(BF16), g_ffn.reshape(1, D_MODEL), wup, cw, wdn,
                g_final.reshape(1, D_MODEL)]
    in_specs = [
        row_spec(D_MODEL),
        row_spec(D_TOK),
        pl.BlockSpec((1, tm, D_MEM), lambda b, t: (b, t, qmem_block)),
        pl.BlockSpec((1, MEM_TOKENS, 2 * D_MEM), lambda b, t: (b, 0, 0)),
        _const_spec((D_MODEL, D_MODEL)),
        _const_spec((1, D_MODEL)),
        _const_spec((D_MODEL, 2 * D_FF)),
        _const_spec((SUBLANES, D_FF)),
        _const_spec((D_FF, D_MODEL)),
        _const_spec((1, D_MODEL)),
    ]
    out_specs = [row_spec(D_MODEL)]
    out_shape = [jax.ShapeDtypeStruct((bsz, seq, D_MODEL), F32)]
    scratch_shapes = [
        pltpu.VMEM((N_FF_CHUNKS, SUBLANES, FF_CHUNK), F32),
        pltpu.VMEM((tm, D_MODEL), F32),
        pltpu.VMEM((tm, D_MODEL), BF16),
        pltpu.VMEM((2, tm, 2 * FF_CHUNK), F32),
        pltpu.VMEM((tm, D_FF), BF16),
    ]
    if next_layer is not None:
        g_mix_n, w_in_n, g_kv, w_kv, w_fgate, b_fgate = next_layer
        wf, bfr, tri = _gate_operands(w_fgate, b_fgate)
        wkvf = jnp.concatenate([w_kv.astype(BF16), wf], axis=1)
        operands += [g_mix_n.reshape(1, D_MODEL), w_in_n.astype(BF16), g_kv.reshape(1, D_MODEL),
                     wkvf, bfr, tri]
        in_specs += [_const_spec((1, D_MODEL)), _const_spec((D_MODEL, D_MODEL)), _const_spec((1, D_MODEL)),
                     _const_spec((D_MODEL, 2 * D_TOK + LANES)), _const_spec((1, LANES)),
                     _const_spec((CUMSUM_BLOCK, CUMSUM_BLOCK))]
        out_specs += [row_spec(D_MODEL), row_spec(2 * D_TOK), row_spec(LANES), row_spec(LANES)]
        out_shape += [jax.ShapeDtypeStruct((bsz, seq, n), BF16) for n in (D_MODEL, 2 * D_TOK, LANES, LANES)]
        scratch_shapes.append(pltpu.VMEM((1, LANES), F32))
    outs = pl.pallas_call(
        functools.partial(_mix_ffn_kernel, final_norm=final_norm, emit_next=next_layer is not None),
        grid=(bsz, seq // tm),
        in_specs=in_specs,
        out_specs=out_specs,
        out_shape=out_shape,
        scratch_shapes=scratch_shapes,
        compiler_params=pltpu.CompilerParams(dimension_semantics=("arbitrary", "arbitrary"),
                                             vmem_limit_bytes=VMEM_LIMIT),
        name="mix_ffn",
    )(*operands)
    return outs if next_layer is not None else outs[0]


def kernel(x, mem, g_mix, w_in, w_out, g_mem, w_mem_kv, s5_a_re, s5_a_im, s5_log_dt, s5_b_re, s5_b_im,
           s5_c_re, s5_c_im, s5_d, w_glu, g_kv, w_kv, w_fgate, b_fgate, g_ffn, w_ffn_up, conv_w, conv_b,
           w_ffn_down, g_final):
    depth = w_in.shape[0]
    assert depth == 2, "one S5 layer followed by one attention layer"
    memkv = _memkv(mem, g_mem, w_mem_kv)

    def ffn_args(l):
        return (memkv[l], w_out[l], g_ffn[l], w_ffn_up[l], conv_w[l], conv_b[l], w_ffn_down[l], g_final)

    tok, qmem = _s5_glu(x, g_mix[0], w_in[0], s5_a_re[0], s5_a_im[0], s5_log_dt[0], s5_b_re[0],
                        s5_b_im[0], s5_c_re[0], s5_c_im[0], s5_d[0], w_glu[0])
    q_scale = jnp.where(jnp.arange(D_MODEL) < D_TOK, (HEAD_DIM ** -0.5) * LOG2E, 1.0).astype(F32)
    h, proj, kv, qb, kb = _mix_ffn(x, tok, qmem, *ffn_args(0), final_norm=False,
                                   next_layer=(g_mix[1], w_in[1] * q_scale[None, :], g_kv, w_kv,
                                               w_fgate, b_fgate))
    tok = _fox(proj, kv, qb, kb)
    return _mix_ffn(h, tok, proj, *ffn_args(1), final_norm=True)
```

```python
import functools
import math

import jax
import jax.numpy as jnp
from jax import lax
from jax.experimental import pallas as pl
from jax.experimental.pallas import tpu as pltpu

F32 = jnp.float32
BF16 = jnp.bfloat16

D_MODEL = 1024
HEAD_DIM = 64
D_MEM = 256
N_MEM_HEADS = 4
MEM_TOKENS = 256
D_TOK = D_MODEL - D_MEM
S5_GROUP = 16
S5_GROUPS = D_TOK // S5_GROUP
S5_STATE = 64
N_FOX_HEADS = D_TOK // HEAD_DIM
D_FF = 2816
EPS = 1e-6
LOG2E = math.log2(math.e)

LANES = 128
SUBLANES = 8
MXU_DIM = 256

ROW_TILE = 512
S5_STEPS = 64
S5_SLABS = D_TOK // MXU_DIM
S5_SLAB_STATE = (MXU_DIM // S5_GROUP) * S5_STATE
S5_STATE_ALL = S5_GROUPS * S5_STATE
S5_STAGES = 2
SCAN_LANES = 512
FF_CHUNK = 256
N_FF_CHUNKS = D_FF // FF_CHUNK
FOX_TQ = 256
FOX_TK = 512
FOX_UNROLL = 16
FOX_UNROLL_DIAG = 14
GATE_COLS = 8
CUMSUM_BLOCK = MXU_DIM
VMEM_LIMIT = 56 * 1024 * 1024


def _rmsnorm(x, g):
    ms = jnp.mean(x * x, axis=-1, keepdims=True)
    return x * lax.rsqrt(ms + EPS) * g


def _const_spec(shape):
    zeros = (0,) * len(shape)
    return pl.BlockSpec(shape, lambda *_: zeros, pipeline_mode=pl.Buffered(1))


def _memkv_kernel(mem_ref, g_ref, w_ref, o_ref):
    mn = _rmsnorm(mem_ref[0], g_ref[...]).astype(BF16)
    for l in range(w_ref.shape[0]):
        o_ref[l, 0] = jnp.dot(mn, w_ref[l], preferred_element_type=F32).astype(BF16)


def _memkv(mem, g_mem, w_mem_kv):
    bsz = mem.shape[0]
    depth = w_mem_kv.shape[0]
    return pl.pallas_call(
        _memkv_kernel,
        grid=(bsz,),
        in_specs=[
            pl.BlockSpec((1, MEM_TOKENS, D_MODEL), lambda b: (b, 0, 0)),
            _const_spec((1, D_MODEL)),
            _const_spec((depth, D_MODEL, 2 * D_MEM)),
        ],
        out_specs=pl.BlockSpec((depth, 1, MEM_TOKENS, 2 * D_MEM), lambda b: (0, b, 0, 0)),
        out_shape=jax.ShapeDtypeStruct((depth, bsz, MEM_TOKENS, 2 * D_MEM), BF16),
        compiler_params=pltpu.CompilerParams(dimension_semantics=("arbitrary",)),
        name="memkv",
    )(mem, g_mem.reshape(1, D_MODEL), w_mem_kv.astype(BF16))


def _s5_discretise(a_re, a_im, log_dt, b_re, b_im, c_re, c_im):
    dt = jnp.exp(log_dt.astype(F32))[:, None]
    lam_re = jnp.minimum(a_re.astype(F32), -1e-4)
    lam_im = a_im.astype(F32)
    mag = jnp.exp(lam_re * dt)
    ph = lam_im * dt
    ab_re, ab_im = mag * jnp.cos(ph), mag * jnp.sin(ph)
    den = lam_re * lam_re + lam_im * lam_im
    z_re = ((ab_re - 1.0) * lam_re + ab_im * lam_im) / den
    z_im = (ab_im * lam_re - (ab_re - 1.0) * lam_im) / den
    br, bi = b_re.astype(F32), b_im.astype(F32)
    bb_re = z_re[..., None] * br - z_im[..., None] * bi
    bb_im = z_re[..., None] * bi + z_im[..., None] * br
    gps = MXU_DIM // S5_GROUP
    eye = jnp.eye(gps, dtype=F32)

    def pack_b(bb):
        bb = bb.reshape(S5_SLABS, gps, S5_STATE, S5_GROUP)
        return jnp.einsum('sgpi,gh->sgihp', bb, eye).reshape(S5_SLABS, MXU_DIM, S5_SLAB_STATE)

    def pack_c(cc):
        cc = cc.reshape(S5_SLABS, gps, S5_GROUP, S5_STATE)
        return jnp.einsum('sgip,gh->sgphi', cc, eye).reshape(S5_SLABS, S5_SLAB_STATE, MXU_DIM)

    b_bd = jnp.concatenate([pack_b(bb_re), pack_b(bb_im)], axis=2).astype(BF16)
    c_bd = jnp.concatenate([pack_c(c_re.astype(F32)), -pack_c(c_im.astype(F32))], axis=1).astype(BF16)
    return ab_re.reshape(1, S5_STATE_ALL), ab_im.reshape(1, S5_STATE_ALL), b_bd, c_bd


def _s5_glu_kernel(x_ref, gmix_ref, win_ref, perm_ref, permt_ref, bbd_ref, cbd_ref, are_ref, aim_ref,
                   d_ref, wglu_ref, o_ref, qm_ref, hre_ref, him_ref, *bufs):
    rows = SUBLANES * S5_STEPS
    sre_refs, sim_refs, up_refs = bufs[0:S5_STAGES], bufs[S5_STAGES:2 * S5_STAGES], bufs[2 * S5_STAGES:]
    i = pl.program_id(0)

    @pl.when(i == 0)
    def _():
        hre_ref[...] = jnp.zeros_like(hre_ref)
        him_ref[...] = jnp.zeros_like(him_ref)
        for ref in bufs:
            ref[...] = jnp.zeros_like(ref)

    def permute_in(up_ref):
        xn = _rmsnorm(x_ref[...].reshape(rows, D_MODEL), gmix_ref[...]).astype(BF16)
        proj = jnp.dot(xn, win_ref[...], preferred_element_type=F32).astype(BF16)
        qm_ref[...] = proj[:, D_TOK:].reshape(SUBLANES, S5_STEPS, D_MEM)
        up_ref[...] = jnp.dot(perm_ref[...], proj[:, :D_TOK], preferred_element_type=F32).astype(BF16)

    def project_in(s, sre_ref, sim_ref, up_ref):
        bu = jnp.dot(up_ref[:, s * MXU_DIM:(s + 1) * MXU_DIM], bbd_ref[s], preferred_element_type=F32)
        sre_ref[:, s * S5_SLAB_STATE:(s + 1) * S5_SLAB_STATE] = bu[:, :S5_SLAB_STATE]
        sim_ref[:, s * S5_SLAB_STATE:(s + 1) * S5_SLAB_STATE] = bu[:, S5_SLAB_STATE:]

    def scan(c, sre_ref, sim_ref):
        cs = slice(c * SCAN_LANES, (c + 1) * SCAN_LANES)
        ar = jnp.broadcast_to(are_ref[:, cs], (SUBLANES, SCAN_LANES))
        ai = jnp.broadcast_to(aim_ref[:, cs], (SUBLANES, SCAN_LANES))
        hr = hre_ref[:, cs]
        hi = him_ref[:, cs]
        for t in range(S5_STEPS):
            rs = slice(t * SUBLANES, (t + 1) * SUBLANES)
            br = sre_ref[rs, cs]
            bi = sim_ref[rs, cs]
            hr, hi = ar * hr - ai * hi + br, ar * hi + ai * hr + bi
            sre_ref[rs, cs] = hr
            sim_ref[rs, cs] = hi
        hre_ref[:, cs] = hr
        him_ref[:, cs] = hi

    def project_out(s, sre_ref, sim_ref):
        ss = slice(s * S5_SLAB_STATE, (s + 1) * S5_SLAB_STATE)
        y_re = jnp.dot(sre_ref[:, ss].astype(BF16), cbd_ref[s, :S5_SLAB_STATE], preferred_element_type=F32)
        y_im = jnp.dot(sim_ref[:, ss].astype(BF16), cbd_ref[s, S5_SLAB_STATE:], preferred_element_type=F32)
        return y_re + y_im

    def gate_out(ys, up_ref):
        y = jnp.concatenate(ys, axis=1) + d_ref[...] * up_ref[...].astype(F32)
        toks = []
        for half in range(2):
            g = jax.nn.gelu(y[half * rows // 2:(half + 1) * rows // 2])
            z = jnp.dot(g.astype(BF16), wglu_ref[...], preferred_element_type=F32)
            toks.append((g * jax.nn.sigmoid(z)).astype(BF16))
        tok = jnp.concatenate(toks, axis=0)
        out = jnp.dot(permt_ref[...], tok, preferred_element_type=F32).astype(BF16)
        o_ref[...] = out.reshape(SUBLANES, S5_STEPS, D_TOK)

    scans_per_slab = S5_STATE_ALL // SCAN_LANES // S5_SLABS
    for r in range(S5_STAGES):
        @pl.when(i % S5_STAGES == r)
        def _(r=r):
            a, b = r, (r - 1) % S5_STAGES
            permute_in(up_refs[a])
            ys = []
            for s in range(S5_SLABS):
                project_in(s, sre_refs[a], sim_refs[a], up_refs[a])
                for k in range(scans_per_slab):
                    scan(s * scans_per_slab + k, sre_refs[b], sim_refs[b])
                ys.append(project_out(s, sre_refs[b], sim_refs[b]))
            gate_out(ys, up_refs[b])


def _s5_glu(h, g_mix, w_in, a_re, a_im, log_dt, b_re, b_im, c_re, c_im, d_skip, w_glu):
    bsz, seq, _ = h.shape
    assert bsz == SUBLANES, "the scan keeps one batch element per sublane"
    rows = SUBLANES * S5_STEPS
    n_blocks = seq // S5_STEPS
    lag = S5_STAGES - 1
    ab_re, ab_im, b_bd, c_bd = _s5_discretise(a_re, a_im, log_dt, b_re, b_im, c_re, c_im)
    src = (jnp.arange(rows) % SUBLANES) * S5_STEPS + jnp.arange(rows) // SUBLANES
    perm = (src[:, None] == jnp.arange(rows)[None, :]).astype(BF16)
    stage_bufs = ([pltpu.VMEM((rows, S5_STATE_ALL), F32)] * (2 * S5_STAGES)
                  + [pltpu.VMEM((rows, D_TOK), BF16)] * S5_STAGES)
    return pl.pallas_call(
        _s5_glu_kernel,
        grid=(n_blocks + lag,),
        in_specs=[
            pl.BlockSpec((bsz, S5_STEPS, D_MODEL), lambda i: (0, jnp.minimum(i, n_blocks - 1), 0)),
            _const_spec((1, D_MODEL)),
            _const_spec((D_MODEL, D_MODEL)),
            _const_spec((rows, rows)),
            _const_spec((rows, rows)),
            _const_spec((S5_SLABS, MXU_DIM, 2 * S5_SLAB_STATE)),
            _const_spec((S5_SLABS, 2 * S5_SLAB_STATE, MXU_DIM)),
            _const_spec((1, S5_STATE_ALL)),
            _const_spec((1, S5_STATE_ALL)),
            _const_spec((1, D_TOK)),
            _const_spec((D_TOK, D_TOK)),
        ],
        out_specs=[
            pl.BlockSpec((bsz, S5_STEPS, D_TOK), lambda i: (0, jnp.maximum(i - lag, 0), 0)),
            pl.BlockSpec((bsz, S5_STEPS, D_MEM), lambda i: (0, jnp.minimum(i, n_blocks - 1), 0)),
        ],
        out_shape=[
            jax.ShapeDtypeStruct((bsz, seq, D_TOK), BF16),
            jax.ShapeDtypeStruct((bsz, seq, D_MEM), BF16),
        ],
        scratch_shapes=[
            pltpu.VMEM((SUBLANES, S5_STATE_ALL), F32),
            pltpu.VMEM((SUBLANES, S5_STATE_ALL), F32),
        ] + stage_bufs,
        compiler_params=pltpu.CompilerParams(dimension_semantics=("arbitrary",),
                                             vmem_limit_bytes=VMEM_LIMIT),
        name="s5_glu",
    )(h, g_mix.reshape(1, D_MODEL), w_in.astype(BF16), perm, perm.T, b_bd, c_bd, ab_re, ab_im,
      d_skip.reshape(1, D_TOK).astype(F32), w_glu.astype(BF16))


def _split3(x):
    hi = x.astype(BF16)
    r = x - hi.astype(F32)
    mid = r.astype(BF16)
    lo = (r - mid.astype(F32)).astype(BF16)
    return hi, mid, lo


def _gate_columns(z, tri_ref, carry_ref):
    n_rows = z.shape[0]
    logf = -(jnp.maximum(-z, 0.0) + jnp.log1p(jnp.exp(-jnp.abs(z))))
    tri = tri_ref[...]
    parts = jnp.concatenate(_split3(logf), axis=1)
    running = carry_ref[...]
    blocks = []
    for r in range(n_rows // CUMSUM_BLOCK):
        c3 = jnp.dot(tri, parts[r * CUMSUM_BLOCK:(r + 1) * CUMSUM_BLOCK], preferred_element_type=F32)
        csum = c3[:, :LANES] + c3[:, LANES:2 * LANES] + c3[:, 2 * LANES:]
        blocks.append(csum + running)
        running = blocks[-1][CUMSUM_BLOCK - 1:CUMSUM_BLOCK, :]
    carry_ref[...] = running
    fcum = jnp.concatenate(blocks, axis=0)
    hi, mid, lo = (part.astype(F32) for part in _split3(fcum * LOG2E))
    lane = lax.broadcasted_iota(jnp.int32, fcum.shape, 1)
    j = lane & (GATE_COLS - 1)
    valid = lane < N_FOX_HEADS * GATE_COLS
    one = jnp.ones_like(hi)
    zero = jnp.zeros_like(hi)
    qb = jnp.where(j == 0, hi, jnp.where(j == 1, mid, jnp.where(j == 2, lo, jnp.where(j < 6, one, zero))))
    kb = jnp.where(j < 3, one, jnp.where(j == 3, -hi, jnp.where(j == 4, -mid, jnp.where(j == 5, -lo, zero))))
    return jnp.where(valid, qb, zero).astype(BF16), jnp.where(valid, kb, zero).astype(BF16)


def _gate_operands(w_fgate, b_fgate):
    pad = LANES - N_FOX_HEADS * GATE_COLS
    wf = jnp.pad(jnp.repeat(w_fgate, GATE_COLS, axis=1), ((0, 0), (0, pad))).astype(BF16)
    bfr = jnp.pad(jnp.repeat(b_fgate, GATE_COLS), (0, pad)).reshape(1, LANES).astype(F32)
    tri = (jnp.arange(CUMSUM_BLOCK)[:, None] >= jnp.arange(CUMSUM_BLOCK)[None, :]).astype(BF16)
    return wf, bfr, tri


def _fox_kernel(itab_ref, ktab_ref, q_ref, qb_ref, k_ref, v_ref, kb_ref, o_ref,
                qq_ref, m_ref, acc_ref, s_ref, p_ref, a_ref, *, kinds):
    hp = pl.program_id(1)
    seq = q_ref.shape[1]
    n_q = seq // FOX_TQ
    rows = 2 * FOX_TQ
    n_steps = len(kinds)

    def scores(t, kw):
        c0 = pl.multiple_of(ktab_ref[t], FOX_TQ)
        kk = jnp.concatenate([k_ref[0, pl.ds(c0, kw), :], kb_ref[0, pl.ds(c0, kw), :]], axis=1)
        return lax.dot_general(qq_ref[itab_ref[t]], kk, (((1,), (1,)), ((), ())), preferred_element_type=F32)

    def lane_blocks(s):
        return [s[:, n * LANES:(n + 1) * LANES] for n in range(s.shape[1] // LANES)]

    def row_max(blocks):
        mx = functools.reduce(jnp.maximum, blocks)
        return jnp.broadcast_to(jnp.max(mx, axis=1, keepdims=True), (rows, LANES))

    def probs(blocks, m):
        return jnp.concatenate([jnp.exp2(blk - m).astype(BF16) for blk in blocks], axis=1)

    def pv_dot(p, t, kw):
        c0 = pl.multiple_of(ktab_ref[t], FOX_TQ)
        vv = jnp.concatenate([v_ref[0, pl.ds(c0, kw), :], jnp.ones((kw, LANES), BF16)], axis=1)
        return jnp.dot(p, vv, preferred_element_type=F32)

    lane = lax.broadcasted_iota(jnp.int32, (FOX_TQ, LANES), 1)
    first = lane < HEAD_DIM
    gate_head = lane // GATE_COLS

    def build_q(i, carry):
        r0 = pl.multiple_of(i * FOX_TQ, FOX_TQ)
        q2 = q_ref[0, pl.ds(r0, FOX_TQ), :].astype(F32)
        qb = qb_ref[0, pl.ds(r0, FOX_TQ), :].astype(F32)
        zero = jnp.zeros_like(q2)
        qa = jnp.concatenate([jnp.where(first, q2, zero), jnp.where(gate_head == 2 * hp, qb, zero)], axis=1)
        qo = jnp.concatenate([jnp.where(first, zero, q2), jnp.where(gate_head == 2 * hp + 1, qb, zero)], axis=1)
        qq_ref[i] = jnp.concatenate([qa, qo], axis=0).astype(BF16)
        return carry

    lax.fori_loop(0, n_q, build_q, 0)

    visible = ((lax.broadcasted_iota(jnp.int32, (rows, FOX_TQ), 0) & (FOX_TQ - 1))
               >= lax.broadcasted_iota(jnp.int32, (rows, FOX_TQ), 1))

    def stage_scores(t, slot, kind):
        kw, _ = kind
        s_ref[slot, :, :kw] = scores(t, kw)

    def load_blocks(slot, kind):
        kw, masked = kind
        s = s_ref[slot, :, :kw]
        if masked:
            assert kw == FOX_TQ
            s = jnp.where(visible, s, -jnp.inf)
        return lane_blocks(s)

    def softmax_update(t, slot, kind, blocks):
        kw, first = kind
        i = itab_ref[t]
        if first:
            m_new = row_max(blocks)
        else:
            m_old = m_ref[i]
            m_new = jnp.maximum(m_old, row_max(blocks))
            a_ref[slot] = jnp.exp2(m_old - m_new)
        m_ref[i] = m_new
        p_ref[slot, :, :kw] = probs(blocks, m_new)

    def stage_pv(t, slot, kind):
        kw, first = kind
        i = itab_ref[t]
        pv = pv_dot(p_ref[slot, :, :kw], t, kw)
        if first:
            acc_ref[i] = pv
        else:
            alpha = a_ref[slot]
            acc_ref[i] = jnp.concatenate([alpha, alpha], axis=1) * acc_ref[i] + pv

    def tick(t, parity, kind2, kind1, kind0):
        assert parity in (0, 1)
        stage_scores(t + 2, parity, kind2)
        blocks = load_blocks(1 - parity, kind1)
        stage_pv(t, parity, kind0)
        softmax_update(t + 1, 1 - parity, kind1, blocks)

    stage_scores(0, 0, kinds[0])
    stage_scores(1, 1, kinds[1])
    softmax_update(0, 0, kinds[0], load_blocks(0, kinds[0]))
    t = 0
    while t < n_steps - 2:
        trio = (kinds[t + 2], kinds[t + 1], kinds[t])
        run = 1
        while t + run < n_steps - 2 and (kinds[t + run + 2], kinds[t + run + 1], kinds[t + run]) == trio:
            run += 1
        unroll = FOX_UNROLL if trio[0][0] == FOX_TK else FOX_UNROLL_DIAG
        looped = (run // unroll) * unroll if t % 2 == 0 else 0
        if looped:
            def body(u, carry, t=t, unroll=unroll, trio=trio):
                for k in range(unroll):
                    tick(t + unroll * u + k, k % 2, *trio)
                return carry

            lax.fori_loop(0, looped // unroll, body, 0)
        for k in range(looped, run):
            tick(t + k, (t + k) % 2, *trio)
        t += run
    last = n_steps - 1
    softmax_update(last, last % 2, kinds[last], load_blocks(last % 2, kinds[last]))
    stage_pv(last - 1, (last - 1) % 2, kinds[last - 1])
    stage_pv(last, last % 2, kinds[last])

    def finish(i, carry):
        acc = acc_ref[i]
        o = acc[:, :LANES] / acc[:, LANES:]
        r0 = pl.multiple_of(i * FOX_TQ, FOX_TQ)
        o_ref[0, pl.ds(r0, FOX_TQ), :] = jnp.where(first, o[:FOX_TQ], o[FOX_TQ:]).astype(BF16)
        return carry

    lax.fori_loop(0, n_q, finish, 0)


def _fox(proj, kv, qb, kb):
    bsz, seq, _ = proj.shape
    n_pairs = D_TOK // LANES
    n_q = seq // FOX_TQ
    rows = 2 * FOX_TQ
    steps = [(i, i * FOX_TQ, (FOX_TQ, True)) for i in range(n_q)]
    steps += [(i, (i * FOX_TQ) // FOX_TK * FOX_TK, (FOX_TQ, False))
              for i in range(n_q) if (i * FOX_TQ) % FOX_TK]
    steps += [(i, j * FOX_TK, (FOX_TK, False))
              for j in range(seq // FOX_TK) for i in range(n_q) if (i * FOX_TQ) // FOX_TK > j]
    assert sum(kw for _, _, (kw, _) in steps) == sum((i + 1) * FOX_TQ for i in range(n_q))
    assert [st[0] for st in steps[:n_q]] == list(range(n_q)) and all(st[2][1] for st in steps[:n_q])
    assert not any(st[2][1] for st in steps[n_q:])
    itab = jnp.asarray([st[0] for st in steps], jnp.int32)
    ktab = jnp.asarray([st[1] for st in steps], jnp.int32)
    grid_spec = pltpu.PrefetchScalarGridSpec(
        num_scalar_prefetch=2,
        grid=(bsz, n_pairs),
        in_specs=[
            pl.BlockSpec((1, seq, LANES), lambda b, p, *_: (b, 0, p)),
            pl.BlockSpec((1, seq, LANES), lambda b, p, *_: (b, 0, 0)),
            pl.BlockSpec((1, seq, LANES), lambda b, p, *_: (b, 0, p)),
            pl.BlockSpec((1, seq, LANES), lambda b, p, *_: (b, 0, n_pairs + p)),
            pl.BlockSpec((1, seq, LANES), lambda b, p, *_: (b, 0, 0)),
        ],
        out_specs=pl.BlockSpec((1, seq, LANES), lambda b, p, *_: (b, 0, p)),
        scratch_shapes=[
            pltpu.VMEM((n_q, rows, 2 * LANES), BF16),
            pltpu.VMEM((n_q, rows, LANES), F32),
            pltpu.VMEM((n_q, rows, 2 * LANES), F32),
            pltpu.VMEM((2, rows, FOX_TK), F32),
            pltpu.VMEM((2, rows, FOX_TK), BF16),
            pltpu.VMEM((2, rows, LANES), F32),
        ],
    )
    return pl.pallas_call(
        functools.partial(_fox_kernel, kinds=tuple(st[2] for st in steps)),
        grid_spec=grid_spec,
        out_shape=jax.ShapeDtypeStruct((bsz, seq, D_TOK), BF16),
        compiler_params=pltpu.CompilerParams(dimension_semantics=("arbitrary", "arbitrary"),
                                             vmem_limit_bytes=VMEM_LIMIT),
        name="fox",
    )(itab, ktab, proj, qb, kv, kv, kb)


def _mix_ffn_kernel(*refs, final_norm, emit_next):
    (h_ref, tok_ref, qm_ref, mkv_ref, wout_ref, gffn_ref, wup_ref, cw_ref, wdn_ref, gfin_ref), refs = (
        refs[:10], refs[10:])
    if emit_next:
        (gmix_ref, win_ref, gkv_ref, wkvf_ref, bf_ref, tri_ref), refs = refs[:6], refs[6:]
        (o_ref, proj_ref, kv_ref, qb_ref, kb_ref), refs = refs[:5], refs[5:]
        carry_ref, acc_ref, xn_ref, up_ref, act_ref, fcarry_ref = refs
    else:
        o_ref, carry_ref, acc_ref, xn_ref, up_ref, act_ref = refs
    tm = h_ref.shape[1]

    qm = qm_ref[0].astype(F32) * ((D_MEM // N_MEM_HEADS) ** -0.5)
    km = mkv_ref[0, :, :D_MEM]
    vm = mkv_ref[0, :, D_MEM:]
    lane = lax.broadcasted_iota(jnp.int32, (tm, D_MEM), 1)
    head = [(lane >= hh * HEAD_DIM) & (lane < (hh + 1) * HEAD_DIM) for hh in range(N_MEM_HEADS)]
    qs = jnp.concatenate([jnp.where(sel, qm, jnp.zeros_like(qm)) for sel in head], axis=0).astype(BF16)
    s = lax.dot_general(qs, km, (((1,), (1,)), ((), ())), preferred_element_type=F32)
    p = jnp.exp(s - jnp.max(s, axis=1, keepdims=True))
    o = jnp.dot(p.astype(BF16), vm, preferred_element_type=F32) / jnp.sum(p, axis=1, keepdims=True)
    mem = jnp.zeros((tm, D_MEM), F32)
    for hh, sel in enumerate(head):
        mem = jnp.where(sel, o[hh * tm:(hh + 1) * tm], mem)

    cat = jnp.concatenate([tok_ref[0], mem.astype(BF16)], axis=1)

    @pl.when(pl.program_id(1) == 0)
    def _():
        carry_ref[...] = jnp.zeros_like(carry_ref)
        if emit_next:
            fcarry_ref[...] = jnp.zeros_like(fcarry_ref)

    def stage_up(c, row_blocks=1):
        for r in range(row_blocks):
            rs = slice(r * tm // row_blocks, (r + 1) * tm // row_blocks)
            for half in range(2):
                cols = slice(half * D_FF + c * FF_CHUNK, half * D_FF + (c + 1) * FF_CHUNK)
                up_ref[c % 2, rs, half * FF_CHUNK:(half + 1) * FF_CHUNK] = jnp.dot(
                    xn_ref[rs], wup_ref[:, cols], preferred_element_type=F32)

    head_blocks = 2
    for r in range(head_blocks):
        rs = slice(r * tm // head_blocks, (r + 1) * tm // head_blocks)
        hmid = h_ref[0, rs] + jnp.dot(cat[rs], wout_ref[...], preferred_element_type=F32)
        acc_ref[rs] = hmid
        xn_ref[rs] = _rmsnorm(hmid, gffn_ref[...]).astype(BF16)

    row8 = lax.broadcasted_iota(jnp.int32, (SUBLANES, FF_CHUNK), 0)
    stage_up(0, head_blocks)
    for c in range(N_FF_CHUNKS):
        if c + 1 < N_FF_CHUNKS:
            stage_up(c + 1)
        a = up_ref[c % 2, :, :FF_CHUNK]
        g = up_ref[c % 2, :, FF_CHUNK:]
        prev = carry_ref[c]
        carry_ref[c] = g[tm - SUBLANES:, :]
        g1 = pltpu.roll(g, 1, 0)
        g2 = pltpu.roll(g, 2, 0)
        g1 = jnp.concatenate([jnp.where(row8 < 1, pltpu.roll(prev, 1, 0), g1[:SUBLANES]), g1[SUBLANES:]], axis=0)
        g2 = jnp.concatenate([jnp.where(row8 < 2, pltpu.roll(prev, 2, 0), g2[:SUBLANES]), g2[SUBLANES:]], axis=0)
        cw = cw_ref[:, c * FF_CHUNK:(c + 1) * FF_CHUNK]
        gc = g2 * cw[0:1] + g1 * cw[1:2] + g * cw[2:3] + cw[3:4]
        act_ref[:, c * FF_CHUNK:(c + 1) * FF_CHUNK] = (gc * jax.nn.sigmoid(gc) * a).astype(BF16)

    out = acc_ref[...] + jnp.dot(act_ref[...], wdn_ref[...], preferred_element_type=F32)
    if final_norm:
        out = _rmsnorm(out, gfin_ref[...])
    o_ref[0] = out
    if emit_next:
        normed = out * lax.rsqrt(jnp.mean(out * out, axis=-1, keepdims=True) + EPS)
        xn_next = (normed * gmix_ref[...]).astype(BF16)
        proj_ref[0] = jnp.dot(xn_next, win_ref[...], preferred_element_type=F32).astype(BF16)
        hs = (normed * gkv_ref[...]).astype(BF16)
        kvz = jnp.dot(hs, wkvf_ref[...], preferred_element_type=F32)
        kv_ref[0] = kvz[:, :2 * D_TOK].astype(BF16)
        qb_ref[0], kb_ref[0] = _gate_columns(kvz[:, 2 * D_TOK:] + bf_ref[...], tri_ref, fcarry_ref)


def _mix_ffn(h, tok, qmem, memkv, w_out, g_ffn, w_up, conv_w, conv_b, w_down, g_final, final_norm,
             next_layer=None):
    bsz, seq, _ = h.shape
    qmem_block = qmem.shape[-1] // D_MEM - 1
    tm = ROW_TILE
    wup = w_up.astype(BF16)
    wdn = w_down.astype(BF16)
    cw = jnp.concatenate([conv_w, conv_b[None, :], jnp.zeros((SUBLANES - 4, D_FF), conv_w.dtype)],
                         axis=0).astype(F32)
    def row_spec(width):
        return pl.BlockSpec((1, tm, width), lambda b, t: (b, t, 0))

    operands = [h, tok, qmem, memkv, w_out.astype(BF16), g_ffn.reshape(1, D_MODEL), wup, cw, wdn,
                g_final.reshape(1, D_MODEL)]
    in_specs = [
        row_spec(D_MODEL),
        row_spec(D_TOK),
        pl.BlockSpec((1, tm, D_MEM), lambda b, t: (b, t, qmem_block)),
        pl.BlockSpec((1, MEM_TOKENS, 2 * D_MEM), lambda b, t: (b, 0, 0)),
        _const_spec((D_MODEL, D_MODEL)),
        _const_spec((1, D_MODEL)),
        _const_spec((D_MODEL, 2 * D_FF)),
        _const_spec((SUBLANES, D_FF)),
        _const_spec((D_FF, D_MODEL)),
        _const_spec((1, D_MODEL)),
    ]
    out_specs = [row_spec(D_MODEL)]
    out_shape = [jax.ShapeDtypeStruct((bsz, seq, D_MODEL), F32)]
    scratch_shapes = [
        pltpu.VMEM((N_FF_CHUNKS, SUBLANES, FF_CHUNK), F32),
        pltpu.VMEM((tm, D_MODEL), F32),
        pltpu.VMEM((tm, D_MODEL), BF16),
        pltpu.VMEM((2, tm, 2 * FF_CHUNK), F32),
        pltpu.VMEM((tm, D_FF), BF16),
    ]
    if next_layer is not None:
        g_mix_n, w_in_n, g_kv, w_kv, w_fgate, b_fgate = next_layer
        wf, bfr, tri = _gate_operands(w_fgate, b_fgate)
        wkvf = jnp.concatenate([w_kv.astype(BF16), wf], axis=1)
        operands += [g_mix_n.reshape(1, D_MODEL), w_in_n.astype(BF16), g_kv.reshape(1, D_MODEL),
                     wkvf, bfr, tri]
        in_specs += [_const_spec((1, D_MODEL)), _const_spec((D_MODEL, D_MODEL)), _const_spec((1, D_MODEL)),
                     _const_spec((D_MODEL, 2 * D_TOK + LANES)), _const_spec((1, LANES)),
                     _const_spec((CUMSUM_BLOCK, CUMSUM_BLOCK))]
        out_specs += [row_spec(D_MODEL), row_spec(2 * D_TOK), row_spec(LANES), row_spec(LANES)]
        out_shape += [jax.ShapeDtypeStruct((bsz, seq, n), BF16) for n in (D_MODEL, 2 * D_TOK, LANES, LANES)]
        scratch_shapes.append(pltpu.VMEM((1, LANES), F32))
    outs = pl.pallas_call(
        functools.partial(_mix_ffn_kernel, final_norm=final_norm, emit_next=next_layer is not None),
        grid=(bsz, seq // tm),
        in_specs=in_specs,
        out_specs=out_specs,
        out_shape=out_shape,
        scratch_shapes=scratch_shapes,
        compiler_params=pltpu.CompilerParams(dimension_semantics=("arbitrary", "arbitrary"),
                                             vmem_limit_bytes=VMEM_LIMIT),
        name="mix_ffn",
    )(*operands)
    return outs if next_layer is not None else outs[0]


def kernel(x, mem, g_mix, w_in, w_out, g_mem, w_mem_kv, s5_a_re, s5_a_im, s5_log_dt, s5_b_re, s5_b_im,
           s5_c_re, s5_c_im, s5_d, w_glu, g_kv, w_kv, w_fgate, b_fgate, g_ffn, w_ffn_up, conv_w, conv_b,
           w_ffn_down, g_final):
    depth = w_in.shape[0]
    assert depth == 2, "one S5 layer followed by one attention layer"
    memkv = _memkv(mem, g_mem, w_mem_kv)

    def ffn_args(l):
        return (memkv[l], w_out[l], g_ffn[l], w_ffn_up[l], conv_w[l], conv_b[l], w_ffn_down[l], g_final)

    tok, qmem = _s5_glu(x, g_mix[0], w_in[0], s5_a_re[0], s5_a_im[0], s5_log_dt[0], s5_b_re[0],
                        s5_b_im[0], s5_c_re[0], s5_c_im[0], s5_d[0], w_glu[0])
    q_scale = jnp.where(jnp.arange(D_MODEL) < D_TOK, (HEAD_DIM ** -0.5) * LOG2E, 1.0).astype(F32)
    h, proj, kv, qb, kb = _mix_ffn(x, tok, qmem, *ffn_args(0), final_norm=False,
                                   next_layer=(g_mix[1], w_in[1] * q_scale[None, :], g_kv, w_kv,
                                               w_fgate, b_fgate))
    tok = _fox(proj, kv, qb, kb)
    return _mix_ffn(h, tok, proj, *ffn_args(1), final_norm=True)
```
